```python
import jax
import jax.numpy as jnp
from jax import lax
import numpy as np

D_MODEL = 1024
BATCH = 8
SEQ = 2048
DEPTH = 2
DEC_BATCH = 128
DEC_SEQ = 8
PAST_LEN = 16384
PAGE_SIZE = 128

N_EVEN = (DEPTH + 1) // 2
N_ODD = DEPTH // 2

H_A = 8
NOPE_A = 64
ROPE_A = 32
V_A = 64
Q_LORA = 384
KV_LORA = 256
ROPE_THETA = 10000.0
MLA_SCALE = (NOPE_A + ROPE_A) ** -0.5
H_B = 4
DK_B = 128
DV_B = 128
H_C = 4
DK_C = 96
DV_C = 192
GATE_RANK = 16
GATE_NORM = 16.0
H_D = 4
DH_D = 64
SB_SCALE = DH_D ** -0.5
D_FF = -((-8 * D_MODEL) // (3 * 256)) * 256

CHUNK = 64
Q_BLOCK = 128
EPS = 1e-6

AB_SPLITS = (Q_LORA, KV_LORA, ROPE_A, H_B * DK_B, H_B * DK_B, H_B * DV_B, H_B * DV_B)
CD_SPLITS = (H_C * DK_C, H_C * DK_C, H_C * DV_C, H_C * DV_C, GATE_RANK, H_D * DH_D, H_D * DH_D, H_D * DH_D)
AB_COLS = sum(AB_SPLITS)
CD_COLS = sum(CD_SPLITS)
MIX_AB = H_A * V_A + H_B * DV_B
MIX_CD = H_C * DV_C + H_D * DH_D

kernel_name = 'hybrid_mla_hgrn2_gla_stickbreak_step'


def _split(z, sizes):
    return jnp.split(z, np.cumsum(sizes)[:-1].tolist(), axis=-1)


def rms_norm(x, w):
    xf = x.astype(jnp.float32)
    y = xf * lax.rsqrt(jnp.mean(xf * xf, axis=-1, keepdims=True) + EPS)
    return (y * w.astype(jnp.float32)).astype(x.dtype)


def rope(x, pos):
    r = x.shape[-1]
    inv = ROPE_THETA ** (-jnp.arange(0, r, 2, dtype=jnp.float32) / r)
    ang = pos[:, None] * inv[None, :]
    cos = jnp.cos(ang)[None, :, None, :]
    sin = jnp.sin(ang)[None, :, None, :]
    xf = x.astype(jnp.float32).reshape(x.shape[:-1] + (r // 2, 2))
    x1, x2 = xf[..., 0], xf[..., 1]
    out = jnp.stack([x1 * cos - x2 * sin, x1 * sin + x2 * cos], axis=-1)
    return out.reshape(x.shape).astype(x.dtype)


def swiglu(u, w_in, w_out):
    g, up = jnp.split(u @ w_in, 2, axis=-1)
    return (jax.nn.silu(g) * up) @ w_out


def gated_linear_chunked(q, k, v, logf, s0):
    B, T, H, _ = q.shape
    DV = v.shape[-1]
    L = min(CHUNK, T)
    n = -(-T // L)
    pad = n * L - T

    def prep(a):
        a = jnp.pad(a.astype(jnp.float32), ((0, 0), (0, pad), (0, 0), (0, 0)))
        return a.reshape(B, n, L, H, a.shape[-1]).transpose(1, 0, 3, 2, 4)

    xs = (prep(q), prep(k), prep(v), prep(logf))
    causal = jnp.tril(jnp.ones((L, L), bool))
    ref = L // 2

    def step(S, inp):
        qb, kb, vb, gb = inp
        G = jnp.cumsum(gb, axis=2)
        Gr = G[:, :, ref:ref + 1]
        a = jnp.einsum('bhtd,bhsd->bhts', qb * jnp.exp(G - Gr), kb * jnp.exp(Gr - G))
        a = jnp.where(causal, a, 0.0)
        o = (jnp.einsum('bhts,bhsv->bhtv', a, vb)
             + jnp.einsum('bhtd,bhdv->bhtv', qb * jnp.exp(G), S))
        GL = G[:, :, -1:]
        S = (jnp.exp(GL[:, :, 0])[..., None] * S
             + jnp.einsum('bhsd,bhsv->bhdv', kb * jnp.exp(GL - G), vb))
        return S, o

    S, o = lax.scan(step, s0.astype(jnp.float32), xs)
    o = o.transpose(1, 0, 3, 2, 4).reshape(B, n * L, H, DV)[:, :T]
    return o, S


def mla_attend_prompt(q_lat, q_pe, ckv, kpe):
    B, T, H, C = q_lat.shape
    QB = min(Q_BLOCK, T)
    nb = T // QB
    f32 = jnp.float32
    qlb = q_lat.astype(f32).reshape(B, nb, QB, H, C).transpose(1, 0, 2, 3, 4)
    qpb = q_pe.astype(f32).reshape(B, nb, QB, H, ROPE_A).transpose(1, 0, 2, 3, 4)
    ck = ckv.astype(f32)
    kp = kpe.astype(f32)
    kpos = jnp.arange(T)

    def block(args):
        i, ql, qp = args
        s = (jnp.einsum('bqhc,bkc->bhqk', ql, ck) + jnp.einsum('bqhr,bkr->bhqk', qp, kp)) * MLA_SCALE
        qpos = i * QB + jnp.arange(QB)
        s = jnp.where(kpos[None, :] <= qpos[:, None], s, -jnp.inf)
        p = jax.nn.softmax(s, axis=-1)
        return jnp.einsum('bhqk,bkc->bqhc', p, ck)

    o = lax.map(block, (jnp.arange(nb), qlb, qpb))
    return o.transpose(1, 0, 2, 3, 4).reshape(B, T, H, C)


def mla_attend_sample(q_lat, q_pe, ckv_new, kpe_new, page_table, cache_ckv, cache_kpe, j):
    f32 = jnp.float32
    Tn = q_lat.shape[1]
    ql = q_lat.astype(f32)
    qp = q_pe.astype(f32)
    cn = ckv_new.astype(f32)
    s = (jnp.einsum('bqhc,bkc->bhqk', ql, cn)
         + jnp.einsum('bqhr,bkr->bhqk', qp, kpe_new.astype(f32))) * MLA_SCALE
    s = jnp.where(jnp.tril(jnp.ones((Tn, Tn), bool)), s, -jnp.inf)
    m = s.max(-1)
    p = jnp.exp(s - m[..., None])
    l = p.sum(-1)
    acc = jnp.einsum('bhqk,bkc->bhqc', p, cn)

    def page_step(carry, phys):
        m, l, acc = carry
        ck = cache_ckv[j, phys].astype(f32)
        kp = cache_kpe[j, phys].astype(f32)
        s = (jnp.einsum('bqhc,bkc->bhqk', ql, ck) + jnp.einsum('bqhr,bkr->bhqk', qp, kp)) * MLA_SCALE
        m_new = jnp.maximum(m, s.max(-1))
        alpha = jnp.exp(m - m_new)
        p = jnp.exp(s - m_new[..., None])
        l = l * alpha + p.sum(-1)
        acc = acc * alpha[..., None] + jnp.einsum('bhqk,bkc->bhqc', p, ck)
        return (m_new, l, acc), None

    (m, l, acc), _ = lax.scan(page_step, (m, l, acc), page_table.T)
    return (acc / l[..., None]).transpose(0, 2, 1, 3)


def sb_weights(z, l_neg, mask, log_rest):
    later = lax.cumsum(l_neg, axis=z.ndim - 1, reverse=True) - l_neg
    return jnp.where(mask, jnp.exp(jax.nn.log_sigmoid(z) + later + log_rest), 0.0)


def sb_attend_prompt(q, k, v):
    B, T, H, Dh = q.shape
    QB = min(Q_BLOCK, T)
    nb = T // QB
    f32 = jnp.float32
    qb = q.astype(f32).reshape(B, nb, QB, H, Dh).transpose(1, 0, 2, 3, 4)
    k32 = k.astype(f32)
    v32 = v.astype(f32)
    kpos = jnp.arange(T)

    def block(args):
        i, qq = args
        z = jnp.einsum('bqhd,bkhd->bhqk', qq, k32) * SB_SCALE
        strict = kpos[None, :] < (i * QB + jnp.arange(QB))[:, None]
        l_neg = jnp.where(strict, jax.nn.log_sigmoid(-z), 0.0)
        A = sb_weights(z, l_neg, strict, 0.0)
        return jnp.einsum('bhqk,bkhd->bqhd', A, v32)

    o = lax.map(block, (jnp.arange(nb), qb))
    return o.transpose(1, 0, 2, 3, 4).reshape(B, T, H, Dh)


def sb_attend_sample(q, k_new, v_new, page_table, cache_k, cache_v, j):
    f32 = jnp.float32
    Tn = q.shape[1]
    qq = q.astype(f32)
    z = jnp.einsum('bqhd,bkhd->bhqk', qq, k_new.astype(f32)) * SB_SCALE
    strict = jnp.tril(jnp.ones((Tn, Tn), bool), -1)
    l_neg = jnp.where(strict, jax.nn.log_sigmoid(-z), 0.0)
    acc = jnp.einsum('bhqk,bkhd->bhqd', sb_weights(z, l_neg, strict, 0.0), v_new.astype(f32))
    rest = l_neg.sum(-1)

    def page_step(carry, phys):
        rest, acc = carry
        kp = cache_k[j, phys].astype(f32)
        vp = cache_v[j, phys].astype(f32)
        z = jnp.einsum('bqhd,bkhd->bhqk', qq, kp) * SB_SCALE
        l_neg = jax.nn.log_sigmoid(-z)
        A = sb_weights(z, l_neg, True, rest[..., None])
        acc = acc + jnp.einsum('bhqk,bkhd->bhqd', A, vp)
        return (rest + l_neg.sum(-1), acc), None

    (rest, acc), _ = lax.scan(page_step, (rest, acc), page_table.T, reverse=True)
    return acc.transpose(0, 2, 1, 3)


def ab_inputs(u, pos, w_in, q_norm_w, w_uq, kv_norm_w, w_uk, lb):
    B, T, _ = u.shape
    f32 = jnp.float32
    cq, ckv, kpe, hq, hf, hi, hg = _split(u @ w_in, AB_SPLITS)
    q = (rms_norm(cq, q_norm_w) @ w_uq).reshape(B, T, H_A, NOPE_A + ROPE_A)
    q_lat = jnp.einsum('bthn,chn->bthc', q[..., :NOPE_A], w_uk)
    q_pe = rope(q[..., NOPE_A:], pos)
    ckv = rms_norm(ckv, kv_norm_w)
    kpe = rope(kpe[:, :, None, :], pos)[:, :, 0]
    f = lb + (1.0 - lb) * jax.nn.sigmoid(hf.astype(f32))
    hk = (1.0 - f).reshape(B, T, H_B, DK_B)
    logf = jnp.log(f).reshape(B, T, H_B, DK_B)
    hq = jax.nn.silu(hq).reshape(B, T, H_B, DK_B)
    hv = hi.reshape(B, T, H_B, DV_B)
    hg = hg.reshape(B, T, H_B, DV_B)
    return q_lat, q_pe, ckv, kpe, hq, hk, hv, logf, hg


def ab_output(o_lat, o_hg, gate, w_uv, norm_w, w_out, dtype):
    B, T = o_lat.shape[:2]
    o_a = jnp.einsum('bthc,chv->bthv', o_lat, w_uv.astype(jnp.float32)).reshape(B, T, H_A * V_A)
    o_b = (rms_norm(o_hg, norm_w) * jax.nn.silu(gate.astype(jnp.float32))).reshape(B, T, H_B * DV_B)
    return jnp.concatenate([o_a, o_b], axis=-1).astype(dtype) @ w_out


def cd_inputs(u, w_in, w_gate2, b_gate2):
    B, T, _ = u.shape
    gq, gk, gv, gg, ga, sq, sk, sv = _split(u @ w_in, CD_SPLITS)
    log_alpha = jax.nn.log_sigmoid((ga @ w_gate2 + b_gate2).astype(jnp.float32)) / GATE_NORM
    return (
        (gq * DK_C ** -0.5).reshape(B, T, H_C, DK_C),
        gk.reshape(B, T, H_C, DK_C),
        gv.reshape(B, T, H_C, DV_C),
        log_alpha.reshape(B, T, H_C, DK_C),
        gg.reshape(B, T, H_C, DV_C),
        sq.reshape(B, T, H_D, DH_D),
        sk.reshape(B, T, H_D, DH_D),
        sv.reshape(B, T, H_D, DH_D),
    )


def cd_output(o_gla, gate, o_sb, norm_w, w_out, dtype):
    B, T = o_gla.shape[:2]
    o_c = (rms_norm(o_gla, norm_w) * jax.nn.silu(gate.astype(jnp.float32))).reshape(B, T, H_C * DV_C)
    o_d = o_sb.reshape(B, T, H_D * DH_D)
    return jnp.concatenate([o_c, o_d], axis=-1).astype(dtype) @ w_out


def trunk(x, pos, P, ctx):
    B = x.shape[0]
    dt = x.dtype
    lb_all = jnp.cumsum(jax.nn.softmax(P['hgrn_lb_logits'].astype(jnp.float32), axis=0), axis=0)
    new = {'ckv': [], 'kpe': [], 'hgrn': [], 'sbk': [], 'sbv': [], 'gla': []}
    h = x
    for layer in range(DEPTH):
        j = layer // 2
        u = rms_norm(h, P['attn_norm_w'][layer])
        if layer % 2 == 0:
            q_lat, q_pe, ckv, kpe, hq, hk, hv, hlogf, hgate = ab_inputs(
                u, pos, P['w_in_ab'][j], P['mla_q_norm_w'][j], P['mla_w_uq'][j],
                P['mla_kv_norm_w'][j], P['mla_w_uk'][j], lb_all[j])
            if ctx is None:
                o_lat = mla_attend_prompt(q_lat, q_pe, ckv, kpe)
                s0 = jnp.zeros((B, H_B, DK_B, DV_B), jnp.float32)
            else:
                o_lat = mla_attend_sample(q_lat, q_pe, ckv, kpe, ctx['page_table'],
                                          ctx['cache_mla_ckv'], ctx['cache_mla_kpe'], j)
                s0 = ctx['state_hgrn'][j]
            o_hg, s_hg = gated_linear_chunked(hq, hk, hv, hlogf, s0)
            mix = ab_output(o_lat, o_hg, hgate, P['mla_w_uv'][j], P['hgrn_norm_w'][j], P['w_out_ab'][j], dt)
            new['ckv'].append(ckv.astype(dt))
            new['kpe'].append(kpe.astype(dt))
            new['hgrn'].append(s_hg.astype(dt))
        else:
            gq, gk, gv, glog, gg, sq, sk, sv = cd_inputs(u, P['w_in_cd'][j], P['gla_w_gate2'][j], P['gla_b_gate2'][j])
            if ctx is None:
                o_sb = sb_attend_prompt(sq, sk, sv)
                s0 = jnp.zeros((B, H_C, DK_C, DV_C), jnp.float32)
            else:
                o_sb = sb_attend_sample(sq, sk, sv, ctx['page_table'], ctx['cache_sb_k'], ctx['cache_sb_v'], j)
                s0 = ctx['state_gla'][j]
            o_gla, s_gla = gated_linear_chunked(gq, gk, gv, glog, s0)
            mix = cd_output(o_gla, gg, o_sb, P['gla_norm_w'][j], P['w_out_cd'][j], dt)
            new['sbk'].append(sk.astype(dt))
            new['sbv'].append(sv.astype(dt))
            new['gla'].append(s_gla.astype(dt))
        h = h + mix
        h = h + swiglu(rms_norm(h, P['ffn_norm_w'][layer]), P['w_ffn_in'][layer], P['w_ffn_out'][layer])
    return rms_norm(h, P['final_norm_w']), new


def setup_inputs(seed: int = 0) -> dict:
    key = jax.random.key(seed)
    ks = iter(jax.random.split(key, 40))
    f32 = jnp.float32
    n_pages = PAST_LEN // PAGE_SIZE
    n_used = DEC_BATCH * n_pages
    n_pool = n_used + n_used // 4

    def nrm(shape, scale=None):
        a = jax.random.normal(next(ks), shape, f32)
        return a if scale is None else a * scale

    def gain(shape):
        return 1.0 + 0.02 * jax.random.normal(next(ks), shape, f32)

    x_prompt = nrm((BATCH, SEQ, D_MODEL))
    x_sample = nrm((DEC_BATCH, DEC_SEQ, D_MODEL))
    cache_mla_ckv = nrm((N_EVEN, n_pool, PAGE_SIZE, KV_LORA))
    cache_mla_kpe = nrm((N_EVEN, n_pool, PAGE_SIZE, ROPE_A))
    state_hgrn = nrm((N_EVEN, DEC_BATCH, H_B, DK_B, DV_B), 0.5)
    cache_sb_k = nrm((N_ODD, n_pool, PAGE_SIZE, H_D, DH_D))
    cache_sb_v = nrm((N_ODD, n_pool, PAGE_SIZE, H_D, DH_D))
    state_gla = nrm((N_ODD, DEC_BATCH, H_C, DK_C, DV_C), 0.5)
    page_table = jax.random.permutation(next(ks), n_pool)[:n_used].reshape(DEC_BATCH, n_pages).astype(jnp.int32)
    return {
        'x_prompt': x_prompt,
        'x_sample': x_sample,
        'cache_mla_ckv': cache_mla_ckv,
        'cache_mla_kpe': cache_mla_kpe,
        'state_hgrn': state_hgrn,
        'cache_sb_k': cache_sb_k,
        'cache_sb_v': cache_sb_v,
        'state_gla': state_gla,
        'page_table': page_table,
        'attn_norm_w': gain((DEPTH, D_MODEL)),
        'ffn_norm_w': gain((DEPTH, D_MODEL)),
        'final_norm_w': gain((D_MODEL,)),
        'w_in_ab': nrm((N_EVEN, D_MODEL, AB_COLS), D_MODEL ** -0.5),
        'mla_q_norm_w': gain((N_EVEN, Q_LORA)),
        'mla_w_uq': nrm((N_EVEN, Q_LORA, H_A * (NOPE_A + ROPE_A)), Q_LORA ** -0.5),
        'mla_kv_norm_w': gain((N_EVEN, KV_LORA)),
        'mla_w_uk': nrm((N_EVEN, KV_LORA, H_A, NOPE_A), KV_LORA ** -0.5),
        'mla_w_uv': nrm((N_EVEN, KV_LORA, H_A, V_A), KV_LORA ** -0.5),
        'hgrn_lb_logits': nrm((N_EVEN + 1, H_B * DK_B), 0.1),
        'hgrn_norm_w': gain((N_EVEN, DV_B)),
        'w_out_ab': nrm((N_EVEN, MIX_AB, D_MODEL), MIX_AB ** -0.5),
        'w_in_cd': nrm((N_ODD, D_MODEL, CD_COLS), D_MODEL ** -0.5),
        'gla_w_gate2': nrm((N_ODD, GATE_RANK, H_C * DK_C), GATE_RANK ** -0.5),
        'gla_b_gate2': nrm((N_ODD, H_C * DK_C), 0.1),
        'gla_norm_w': gain((N_ODD, DV_C)),
        'w_out_cd': nrm((N_ODD, MIX_CD, D_MODEL), MIX_CD ** -0.5),
        'w_ffn_in': nrm((DEPTH, D_MODEL, 2 * D_FF), D_MODEL ** -0.5),
        'w_ffn_out': nrm((DEPTH, D_FF, D_MODEL), D_FF ** -0.5),
    }


def reference(x_prompt, x_sample, cache_mla_ckv, cache_mla_kpe, state_hgrn, cache_sb_k, cache_sb_v,
              state_gla, page_table, attn_norm_w, ffn_norm_w, final_norm_w, w_in_ab, mla_q_norm_w,
              mla_w_uq, mla_kv_norm_w, mla_w_uk, mla_w_uv, hgrn_lb_logits, hgrn_norm_w, w_out_ab,
              w_in_cd, gla_w_gate2, gla_b_gate2, gla_norm_w, w_out_cd, w_ffn_in, w_ffn_out):
    P = {
        'attn_norm_w': attn_norm_w, 'ffn_norm_w': ffn_norm_w, 'final_norm_w': final_norm_w,
        'w_in_ab': w_in_ab, 'mla_q_norm_w': mla_q_norm_w, 'mla_w_uq': mla_w_uq,
        'mla_kv_norm_w': mla_kv_norm_w, 'mla_w_uk': mla_w_uk, 'mla_w_uv': mla_w_uv,
        'hgrn_lb_logits': hgrn_lb_logits, 'hgrn_norm_w': hgrn_norm_w, 'w_out_ab': w_out_ab,
        'w_in_cd': w_in_cd, 'gla_w_gate2': gla_w_gate2, 'gla_b_gate2': gla_b_gate2,
        'gla_norm_w': gla_norm_w, 'w_out_cd': w_out_cd, 'w_ffn_in': w_ffn_in, 'w_ffn_out': w_ffn_out,
    }
    pos_prompt = jnp.arange(x_prompt.shape[1], dtype=jnp.float32)
    past = page_table.shape[1] * PAGE_SIZE
    pos_sample = jnp.arange(x_sample.shape[1], dtype=jnp.float32) + float(past)

    y_prompt, newp = trunk(x_prompt, pos_prompt, P, None)
    ctx = {
        'page_table': page_table, 'cache_mla_ckv': cache_mla_ckv, 'cache_mla_kpe': cache_mla_kpe,
        'state_hgrn': state_hgrn, 'cache_sb_k': cache_sb_k, 'cache_sb_v': cache_sb_v, 'state_gla': state_gla,
    }
    y_sample, news = trunk(x_sample, pos_sample, P, ctx)

    p_ckv = jnp.stack(newp['ckv'])
    p_kpe = jnp.stack(newp['kpe'])
    p_hgrn = jnp.stack(newp['hgrn'])
    p_sbk = jnp.stack(newp['sbk'])
    p_sbv = jnp.stack(newp['sbv'])
    p_gla = jnp.stack(newp['gla'])
    s_ckv = jnp.stack(news['ckv'])
    s_kpe = jnp.stack(news['kpe'])
    s_hgrn = jnp.stack(news['hgrn'])
    s_sbk = jnp.stack(news['sbk'])
    s_sbv = jnp.stack(news['sbv'])
    s_gla = jnp.stack(news['gla'])
    return (y_prompt, y_sample, p_ckv, p_kpe, p_hgrn, p_sbk, p_sbv, p_gla,
            s_ckv, s_kpe, s_hgrn, s_sbk, s_sbv, s_gla)
```

```python
import functools

import jax
import jax.numpy as jnp
from jax import lax
from jax.experimental import pallas as pl
from jax.experimental.pallas import tpu as pltpu

F32 = jnp.float32
BF16 = jnp.bfloat16

D_MODEL = 1024
PAGE = 128
H_A, NOPE_A, ROPE_A, V_A = 8, 64, 32, 64
Q_LORA, KV_LORA = 384, 256
ROPE_THETA = 10000.0
MLA_SCALE = (NOPE_A + ROPE_A) ** -0.5
H_B, DK_B, DV_B = 4, 128, 128
H_C, DK_C, DV_C = 4, 96, 192
DKP_C, DVP_C = 128, 256
GATE_RANK, GATE_RANK_P = 16, 128
GATE_NORM = 16.0
H_D, DH_D = 4, 64
SB_SCALE = DH_D ** -0.5
D_FF = 2816
EPS = 1e-6
CHUNK = 64
ROPE_W = H_A * ROPE_A

VMEM_LIMIT = 56 * 1024 * 1024


def _cparams(*sem):
    return pltpu.CompilerParams(dimension_semantics=sem, vmem_limit_bytes=VMEM_LIMIT)


def _dot(a, b):
    return jnp.dot(a, b, preferred_element_type=F32)


def _dot_nt(a, b):
    return lax.dot_general(a, b, (((1,), (1,)), ((), ())), preferred_element_type=F32)


def _dot_tn(a, b):
    return lax.dot_general(a, b, (((0,), (0,)), ((), ())), preferred_element_type=F32)


def _split_bf16(x, n):
    parts = []
    for _ in range(n - 1):
        p = x.astype(BF16)
        parts.append(p)
        x = x - p.astype(F32)
    parts.append(x.astype(BF16))
    return parts


def _rms(x, w, n=None):
    n = x.shape[-1] if n is None else n
    ms = jnp.sum(x * x, axis=-1, keepdims=True) * (1.0 / n)
    return x * lax.rsqrt(ms + EPS) * w


def _sigmoid(x):
    return 1.0 / (1.0 + jnp.exp(-x))


def _log_sigmoid(x):
    return jnp.minimum(x, 0.0) - jnp.log(1.0 + jnp.exp(-jnp.abs(x)))


def _rope(x, cos, sin_signed):
    n = x.shape[-1]
    lane = lax.broadcasted_iota(jnp.int32, x.shape, x.ndim - 1)
    nxt = pltpu.roll(x, n - 1, x.ndim - 1)
    prv = pltpu.roll(x, 1, x.ndim - 1)
    swap = jnp.where((lane & 1) == 0, nxt, prv)
    return x * cos + swap * sin_signed


def _norm_matmul_body(x_ref, nw_ref, w_ref, *rest, n_out):
    outs, xn_ref = rest[:n_out], rest[n_out]
    j = pl.program_id(1)

    @pl.when(j == 0)
    def _():
        xn_ref[...] = _rms(x_ref[...], nw_ref[...]).astype(BF16)

    z = _dot(xn_ref[...], w_ref[...])
    if n_out == 1:
        outs[0][...] = z
    else:
        for k in range(n_out):
            @pl.when(j == k)
            def _(k=k):
                outs[k][...] = z


def _norm_matmul(x, nw, w, tn, split):
    m, kdim = x.shape
    n = w.shape[1]
    tm = min(512, m)
    nj = n // tn
    n_out = nj if split else 1
    if split:
        out_shape = [jax.ShapeDtypeStruct((m, tn), F32)] * nj
        out_specs = [pl.BlockSpec((tm, tn), lambda i, j: (i, 0))] * nj
    else:
        out_shape = [jax.ShapeDtypeStruct((m, n), F32)]
        out_specs = [pl.BlockSpec((tm, tn), lambda i, j: (i, j))]
    outs = pl.pallas_call(
        functools.partial(_norm_matmul_body, n_out=n_out),
        grid=(m // tm, nj),
        in_specs=[pl.BlockSpec((tm, kdim), lambda i, j: (i, 0)),
                  pl.BlockSpec((1, kdim), lambda i, j: (0, 0)),
                  pl.BlockSpec((kdim, tn), lambda i, j: (0, j))],
        out_specs=out_specs,
        out_shape=out_shape,
        scratch_shapes=[pltpu.VMEM((tm, kdim), BF16)],
        compiler_params=_cparams("parallel", "arbitrary"),
        name="norm_matmul",
    )(x, nw.reshape(1, kdim), w)
    return outs if split else outs[0]


def _mla_prep_body(x_ref, anw_ref, w_ref, qnw_ref, wuq_ref, wk_ref, kvnw_ref, cos_ref, sin_ref,
                   qlat_ref, qpe_ref, ckv_ref, kpe_ref, kpet_ref):
    xn = _rms(x_ref[...], anw_ref[...]).astype(BF16)
    z = _dot(xn, w_ref[...])
    cqn = _rms(z[:, :Q_LORA], qnw_ref[...]).astype(BF16)
    q = _dot(cqn, wuq_ref[...])
    qn = q[:, :H_A * NOPE_A].astype(BF16)
    for h in range(H_A):
        p = h // 2
        ql = _dot(qn[:, 128 * p:128 * (p + 1)], wk_ref[h])
        qlat_ref[h] = (ql * MLA_SCALE).astype(qlat_ref.dtype)
    cos, sin = cos_ref[...], sin_ref[...]
    qr = _rope(q[:, H_A * NOPE_A:], cos, sin) * MLA_SCALE
    qpe_ref[...] = qr.astype(qpe_ref.dtype)
    ckv_ref[...] = _rms(z[:, Q_LORA:Q_LORA + KV_LORA], kvnw_ref[...])
    kr = _rope(z[:, Q_LORA + KV_LORA:], cos, sin)
    kpet_ref[...] = kr.astype(BF16)
    kpe_ref[...] = kr[:, :ROPE_A]


def _mla_prep(x, anw, w_mla, qnw, wuq, wk, kvnw, cos, sin, q_dtype):
    m = x.shape[0]
    tm = min(512, m)
    nt = cos.shape[0] // tm
    wz = w_mla.shape[1]
    return pl.pallas_call(
        _mla_prep_body,
        grid=(m // tm,),
        in_specs=[pl.BlockSpec((tm, D_MODEL), lambda i: (i, 0)),
                  pl.BlockSpec((1, D_MODEL), lambda i: (0, 0)),
                  pl.BlockSpec((D_MODEL, wz), lambda i: (0, 0)),
                  pl.BlockSpec((1, Q_LORA), lambda i: (0, 0)),
                  pl.BlockSpec(wuq.shape, lambda i: (0, 0)),
                  pl.BlockSpec(wk.shape, lambda i: (0, 0, 0)),
                  pl.BlockSpec((1, KV_LORA), lambda i: (0, 0)),
                  pl.BlockSpec((tm, ROPE_W), lambda i: (i % nt, 0)),
                  pl.BlockSpec((tm, ROPE_W), lambda i: (i % nt, 0))],
        out_specs=[pl.BlockSpec((H_A, tm, KV_LORA), lambda i: (0, i, 0)),
                   pl.BlockSpec((tm, ROPE_W), lambda i: (i, 0)),
                   pl.BlockSpec((tm, KV_LORA), lambda i: (i, 0)),
                   pl.BlockSpec((tm, ROPE_A), lambda i: (i, 0)),
                   pl.BlockSpec((tm, ROPE_W), lambda i: (i, 0))],
        out_shape=[jax.ShapeDtypeStruct((H_A, m, KV_LORA), q_dtype),
                   jax.ShapeDtypeStruct((m, ROPE_W), q_dtype),
                   jax.ShapeDtypeStruct((m, KV_LORA), F32),
                   jax.ShapeDtypeStruct((m, ROPE_A), F32),
                   jax.ShapeDtypeStruct((m, ROPE_W), BF16)],
        compiler_params=_cparams("parallel"),
        name="mla_prep",
    )(x, anw.reshape(1, -1), w_mla, qnw.reshape(1, -1), wuq, wk, kvnw.reshape(1, -1), cos, sin)


def _head_stack(x, width, n_heads):
    lane = lax.broadcasted_iota(jnp.int32, x.shape, 1)
    zero = jnp.zeros_like(x)
    return jnp.concatenate(
        [jnp.where((lane >= h * width) & (lane < (h + 1) * width), x, zero) for h in range(n_heads)], axis=0)


def _uv_project(o, wuv_ref, t):
    ob = o.astype(BF16)
    pieces = []
    for p in range(H_A // 2):
        pieces.append(_dot(ob[2 * p * t:(2 * p + 1) * t], wuv_ref[2 * p])
                      + _dot(ob[(2 * p + 1) * t:(2 * p + 2) * t], wuv_ref[2 * p + 1]))
    return jnp.concatenate(pieces, axis=-1)


def _mla_prompt_body(qlat_ref, qpe_ref, ckv_ref, kpet_ref, wuv_ref, o_ref,
                     ckb_ref, m_ref, l_ref, acc_ref, *, tq, tk):
    qi = pl.program_id(1)
    rows = H_A * tq

    @pl.when(qi == 0)
    def _():
        ckb_ref[...] = ckv_ref[...].astype(BF16)

    ql = qlat_ref[...].reshape(rows, KV_LORA)
    qp = _head_stack(qpe_ref[...], ROPE_A, H_A)
    m_ref[...] = jnp.full(m_ref.shape, -jnp.inf, F32)
    l_ref[...] = jnp.zeros(l_ref.shape, F32)
    acc_ref[...] = jnp.zeros(acc_ref.shape, F32)

    def block(k0, masked):
        ck = ckb_ref[pl.ds(k0, tk), :]
        kp = kpet_ref[pl.ds(k0, tk), :]
        s = _dot_nt(ql, ck) + _dot_nt(qp, kp)
        if masked:
            row = lax.broadcasted_iota(jnp.int32, s.shape, 0) & (tq - 1)
            col = lax.broadcasted_iota(jnp.int32, s.shape, 1)
            s = jnp.where(col + k0 <= row + qi * tq, s, -jnp.inf)
        m_prev = m_ref[...]
        m_new = jnp.maximum(m_prev, jnp.max(s, axis=-1, keepdims=True))
        alpha = jnp.exp(m_prev - m_new)
        p = jnp.exp(s - m_new)
        l_ref[...] = alpha * l_ref[...] + jnp.sum(p, axis=-1, keepdims=True)
        acc_ref[...] = alpha * acc_ref[...] + _dot(p.astype(BF16), ck)
        m_ref[...] = m_new

    nfull = (qi * tq) // tk

    def full_block(kb, carry):
        block(pl.multiple_of(kb * tk, tk), False)
        return carry

    lax.fori_loop(0, nfull, full_block, 0)
    block(pl.multiple_of(nfull * tk, tk), True)
    o = acc_ref[...] / l_ref[...]
    o_ref[...] = _uv_project(o, wuv_ref, tq).astype(o_ref.dtype)


def _mla_prompt_attn(qlat, qpe, ckv, kpet, wuv, b, t):
    tq = min(128, t)
    tk = min(256, t)
    nq = t // tq
    rows = H_A * tq
    return pl.pallas_call(
        functools.partial(_mla_prompt_body, tq=tq, tk=tk),
        grid=(b, nq),
        in_specs=[pl.BlockSpec((H_A, tq, KV_LORA), lambda bi, qi: (0, bi * nq + qi, 0)),
                  pl.BlockSpec((tq, ROPE_W), lambda bi, qi: (bi * nq + qi, 0)),
                  pl.BlockSpec((t, KV_LORA), lambda bi, qi: (bi, 0)),
                  pl.BlockSpec((t, ROPE_W), lambda bi, qi: (bi, 0)),
                  pl.BlockSpec(wuv.shape, lambda bi, qi: (0, 0, 0))],
        out_specs=pl.BlockSpec((tq, H_A * V_A), lambda bi, qi: (bi * nq + qi, 0)),
        out_shape=jax.ShapeDtypeStruct((b * t, H_A * V_A), BF16),
        scratch_shapes=[pltpu.VMEM((t, KV_LORA), BF16),
                        pltpu.VMEM((rows, 1), F32),
                        pltpu.VMEM((rows, 1), F32),
                        pltpu.VMEM((rows, KV_LORA), F32)],
        compiler_params=_cparams("parallel", "arbitrary"),
        name="mla_prompt_attn",
    )(qlat, qpe, ckv, kpet, wuv)


def _mla_sample_body(pt_ref, qlat_ref, qpe_ref, ckn_ref, kpn_ref, wuv_ref, *rest, tn, g):
    ck_refs, kp_refs = rest[:g], rest[g:2 * g]
    o_ref, ql_ref, qp_ref, ckpad_ref, kppad_ref, m_ref, l_ref, acc_ref = rest[2 * g:]
    step = pl.program_id(1)
    rows = H_A * tn

    def attend(ck32, kp32, carry, mask=None):
        m_prev, l_prev, acc = carry
        ck = ck32.astype(BF16)
        s = _dot_nt(ql_ref[...], ck) + _dot_nt(qp_ref[...], kp32.astype(BF16))
        if mask is not None:
            s = jnp.where(mask, s, -jnp.inf)
        m_new = jnp.maximum(m_prev, jnp.max(s, axis=-1, keepdims=True))
        alpha = jnp.exp(m_prev - m_new)
        p = jnp.exp(s - m_new)
        l_new = alpha * l_prev + jnp.sum(p, axis=-1, keepdims=True)
        return m_new, l_new, alpha * acc + _dot(p.astype(BF16), ck)

    @pl.when(step == 0)
    def _():
        ql_ref[...] = qlat_ref[...].reshape(rows, KV_LORA).astype(BF16)
        qps = _head_stack(qpe_ref[...], ROPE_A, H_A).astype(BF16)
        fr = lax.broadcasted_iota(jnp.int32, (ROPE_W, ROPE_A), 0)
        fc = lax.broadcasted_iota(jnp.int32, (ROPE_W, ROPE_A), 1)
        fold = jnp.where((fr & (ROPE_A - 1)) == fc, 1.0, 0.0).astype(BF16)
        qp_ref[...] = _dot(qps, fold).astype(BF16)
        ckpad_ref[...] = jnp.zeros(ckpad_ref.shape, F32)
        kppad_ref[...] = jnp.zeros(kppad_ref.shape, F32)
        ckpad_ref[0:tn, :] = ckn_ref[...]
        kppad_ref[0:tn, :] = kpn_ref[...]
        row = lax.broadcasted_iota(jnp.int32, (rows, PAGE), 0) & (tn - 1)
        col = lax.broadcasted_iota(jnp.int32, (rows, PAGE), 1)
        init = (jnp.full((rows, 1), -jnp.inf, F32), jnp.zeros((rows, 1), F32),
                jnp.zeros((rows, KV_LORA), F32))
        m0, l0, a0 = attend(ckpad_ref[...], kppad_ref[...], init, mask=col <= row)
        m_ref[...], l_ref[...], acc_ref[...] = m0, l0, a0

    carry = (m_ref[...], l_ref[...], acc_ref[...])
    for i in range(g):
        carry = attend(ck_refs[i][...], kp_refs[i][...], carry)
    m_ref[...], l_ref[...], acc_ref[...] = carry

    @pl.when(step == pl.num_programs(1) - 1)
    def _():
        o = acc_ref[...] / l_ref[...]
        o_ref[...] = _uv_project(o, wuv_ref, tn).astype(o_ref.dtype)


def _mla_sample_attn(page_table, qlat, qpe, ckn, kpn, wuv, cache_ckv, cache_kpe, tn, g):
    bs, n_pages = page_table.shape
    rows = H_A * tn
    pt = page_table.reshape(-1)

    def page_map(i):
        return lambda b, s, pt_ref: (pt_ref[b * n_pages + s * g + i], 0, 0)

    in_specs = [pl.BlockSpec((H_A, tn, KV_LORA), lambda b, s, pt_ref: (0, b, 0)),
                pl.BlockSpec((tn, ROPE_W), lambda b, s, pt_ref: (b, 0)),
                pl.BlockSpec((tn, KV_LORA), lambda b, s, pt_ref: (b, 0)),
                pl.BlockSpec((tn, ROPE_A), lambda b, s, pt_ref: (b, 0)),
                pl.BlockSpec(wuv.shape, lambda b, s, pt_ref: (0, 0, 0))]
    in_specs += [pl.BlockSpec((None, PAGE, KV_LORA), page_map(i)) for i in range(g)]
    in_specs += [pl.BlockSpec((None, PAGE, ROPE_A), page_map(i)) for i in range(g)]
    return pl.pallas_call(
        functools.partial(_mla_sample_body, tn=tn, g=g),
        grid_spec=pltpu.PrefetchScalarGridSpec(
            num_scalar_prefetch=1,
            grid=(bs, n_pages // g),
            in_specs=in_specs,
            out_specs=pl.BlockSpec((tn, H_A * V_A), lambda b, s, pt_ref: (b, 0)),
            scratch_shapes=[pltpu.VMEM((rows, KV_LORA), BF16),
                            pltpu.VMEM((rows, ROPE_A), BF16),
                            pltpu.VMEM((PAGE, KV_LORA), F32),
                            pltpu.VMEM((PAGE, ROPE_A), F32),
                            pltpu.VMEM((rows, 1), F32),
                            pltpu.VMEM((rows, 1), F32),
                            pltpu.VMEM((rows, KV_LORA), F32)]),
        out_shape=jax.ShapeDtypeStruct((bs * tn, H_A * V_A), F32),
        compiler_params=_cparams("parallel", "arbitrary"),
        name="mla_sample_attn",
    )(pt, qlat, qpe, ckn, kpn, wuv, *([cache_ckv] * g), *([cache_kpe] * g))


def _sb_block(qs, k, v, rest, acc, upper, mask=None):
    z = _dot_nt(qs, k)
    l_neg = _log_sigmoid(-z)
    if mask is not None:
        l_neg = jnp.where(mask, l_neg, 0.0)
    hi, lo = _split_bf16(l_neg, 2)
    later = _dot(hi, upper) + _dot(lo, upper) + rest
    a = jnp.exp(z + l_neg + later)
    if mask is not None:
        a = jnp.where(mask, a, 0.0)
    acc = acc + _dot(a.astype(BF16), v)
    rest = rest + jnp.sum(l_neg, axis=-1, keepdims=True)
    return rest, acc


def _upper_ones(n):
    r = lax.broadcasted_iota(jnp.int32, (n, n), 0)
    c = lax.broadcasted_iota(jnp.int32, (n, n), 1)
    return jnp.where(r > c, 1.0, 0.0).astype(BF16)


def _head_collapse(acc, t, width, n_heads):
    lane = lax.broadcasted_iota(jnp.int32, (t, n_heads * width), 1)
    out = jnp.zeros((t, n_heads * width), F32)
    for h in range(n_heads):
        out = out + jnp.where((lane >= h * width) & (lane < (h + 1) * width), acc[h * t:(h + 1) * t], 0.0)
    return out


def _sb_prompt_body(q_ref, k_ref, v_ref, o_ref, kb_ref, vb_ref, rest_ref, acc_ref, *, tq, tk):
    qi = pl.program_id(1)
    rows = H_D * tq

    @pl.when(qi == 0)
    def _():
        kb_ref[...] = k_ref[...].astype(BF16)
        vb_ref[...] = v_ref[...].astype(BF16)

    qs = _head_stack(q_ref[...] * SB_SCALE, DH_D, H_D).astype(BF16)
    upper = _upper_ones(tk)
    rest_ref[...] = jnp.zeros(rest_ref.shape, F32)
    acc_ref[...] = jnp.zeros(acc_ref.shape, F32)

    def block(k0, masked):
        mask = None
        if masked:
            row = lax.broadcasted_iota(jnp.int32, (rows, tk), 0) & (tq - 1)
            col = lax.broadcasted_iota(jnp.int32, (rows, tk), 1)
            mask = col + k0 < row + qi * tq
        rest, acc = _sb_block(qs, kb_ref[pl.ds(k0, tk), :], vb_ref[pl.ds(k0, tk), :],
                              rest_ref[...], acc_ref[...], upper, mask)
        rest_ref[...] = rest
        acc_ref[...] = acc

    nfull = (qi * tq) // tk
    block(pl.multiple_of(nfull * tk, tk), True)

    def full_block(i, carry):
        block(pl.multiple_of((nfull - 1 - i) * tk, tk), False)
        return carry

    lax.fori_loop(0, nfull, full_block, 0)
    o_ref[...] = _head_collapse(acc_ref[...], tq, DH_D, H_D).astype(o_ref.dtype)


def _sb_prompt_attn(q, k, v, b, t):
    tq = min(128, t)
    tk = min(256, t)
    nq = t // tq
    rows = H_D * tq
    w = H_D * DH_D
    return pl.pallas_call(
        functools.partial(_sb_prompt_body, tq=tq, tk=tk),
        grid=(b, nq),
        in_specs=[pl.BlockSpec((tq, w), lambda bi, qi: (bi * nq + qi, 0)),
                  pl.BlockSpec((t, w), lambda bi, qi: (bi, 0)),
                  pl.BlockSpec((t, w), lambda bi, qi: (bi, 0))],
        out_specs=pl.BlockSpec((tq, w), lambda bi, qi: (bi * nq + qi, 0)),
        out_shape=jax.ShapeDtypeStruct((b * t, w), BF16),
        scratch_shapes=[pltpu.VMEM((t, w), BF16),
                        pltpu.VMEM((t, w), BF16),
                        pltpu.VMEM((rows, 1), F32),
                        pltpu.VMEM((rows, w), F32)],
        compiler_params=_cparams("parallel", "arbitrary"),
        name="sb_prompt_attn",
    )(q, k, v)


def _sb_sample_body(pt_ref, q_ref, kn_ref, vn_ref, *rest, tn, g):
    k_refs, v_refs = rest[:g], rest[g:2 * g]
    o_ref, qs_ref, kpad_ref, vpad_ref, rest_ref, acc_ref = rest[2 * g:]
    step = pl.program_id(1)
    rows = H_D * tn
    upper = _upper_ones(PAGE)

    @pl.when(step == 0)
    def _():
        qs_ref[...] = _head_stack(q_ref[...] * SB_SCALE, DH_D, H_D).astype(BF16)
        kpad_ref[...] = jnp.zeros(kpad_ref.shape, F32)
        vpad_ref[...] = jnp.zeros(vpad_ref.shape, F32)
        kpad_ref[0:tn, :] = kn_ref[...]
        vpad_ref[0:tn, :] = vn_ref[...]
        row = lax.broadcasted_iota(jnp.int32, (rows, PAGE), 0) & (tn - 1)
        col = lax.broadcasted_iota(jnp.int32, (rows, PAGE), 1)
        r0, a0 = _sb_block(qs_ref[...], kpad_ref[...].astype(BF16), vpad_ref[...].astype(BF16),
                           jnp.zeros((rows, 1), F32), jnp.zeros((rows, H_D * DH_D), F32), upper,
                           mask=col < row)
        rest_ref[...], acc_ref[...] = r0, a0

    r, a = rest_ref[...], acc_ref[...]
    qs = qs_ref[...]
    for i in range(g):
        r, a = _sb_block(qs, k_refs[i][...].astype(BF16), v_refs[i][...].astype(BF16), r, a, upper)
    rest_ref[...], acc_ref[...] = r, a

    @pl.when(step == pl.num_programs(1) - 1)
    def _():
        o_ref[...] = _head_collapse(acc_ref[...], tn, DH_D, H_D).astype(o_ref.dtype)


def _sb_sample_attn(page_table, q, kn, vn, cache_k, cache_v, tn, g):
    bs, n_pages = page_table.shape
    rows = H_D * tn
    w = H_D * DH_D
    pt = page_table.reshape(-1)

    def page_map(i):
        return lambda b, s, pt_ref: (pt_ref[b * n_pages + n_pages - 1 - (s * g + i)], 0, 0)

    in_specs = [pl.BlockSpec((tn, w), lambda b, s, pt_ref: (b, 0))] * 3
    in_specs += [pl.BlockSpec((None, PAGE, w), page_map(i)) for i in range(g)] * 2
    return pl.pallas_call(
        functools.partial(_sb_sample_body, tn=tn, g=g),
        grid_spec=pltpu.PrefetchScalarGridSpec(
            num_scalar_prefetch=1,
            grid=(bs, n_pages // g),
            in_specs=in_specs,
            out_specs=pl.BlockSpec((tn, w), lambda b, s, pt_ref: (b, 0)),
            scratch_shapes=[pltpu.VMEM((rows, w), BF16),
                            pltpu.VMEM((PAGE, w), F32),
                            pltpu.VMEM((PAGE, w), F32),
                            pltpu.VMEM((rows, 1), F32),
                            pltpu.VMEM((rows, w), F32)]),
        out_shape=jax.ShapeDtypeStruct((bs * tn, w), F32),
        compiler_params=_cparams("parallel", "arbitrary"),
        name="sb_sample_attn",
    )(pt, q, kn, vn, *([cache_k] * g), *([cache_v] * g))


def _gated_body(*refs, mode, lq, nblk, dk, dv, dkp, dvp, has_s0):
    nh = 4
    nsq = CHUNK // lq
    it = iter(refs)
    if mode == "hgrn":
        zq_ref, zf_ref, zv_ref, zg_ref, lb_ref, nw_ref = (next(it) for _ in range(6))
    else:
        zq_ref, zk_ref, zv_ref, zg_ref, ga_ref, wg_hi_ref, wg_lo_ref, bg_ref, nw_ref = (next(it) for _ in range(9))
    s0_ref = next(it) if has_s0 else None
    o_ref, sout_ref, s_ref = next(it), next(it), next(it)
    step = pl.program_id(1)

    if has_s0:
        s_ref[...] = jnp.zeros(s_ref.shape, F32)
        s_ref[:, :, 0:dk, 0:dv] = s0_ref[...]
    else:
        @pl.when(step == 0)
        def _():
            s_ref[...] = jnp.zeros(s_ref.shape, F32)

    r = lax.broadcasted_iota(jnp.int32, (CHUNK, CHUNK), 0)
    c = lax.broadcasted_iota(jnp.int32, (CHUNK, CHUNK), 1)
    seq_r, seq_c = r // lq, c // lq
    same = seq_r == seq_c
    causal = same & (r >= c)
    ref_pos = seq_r * lq + lq // 2
    ones = lambda m: jnp.where(m, 1.0, 0.0).astype(BF16)
    sums = jnp.concatenate([ones(causal), ones(same & (c <= ref_pos)), ones(same)], axis=0)
    row_seq = lax.broadcasted_iota(jnp.int32, (CHUNK, dkp), 0) // lq

    def chunk(ci):
        rows = pl.ds(pl.multiple_of(ci * CHUNK, CHUNK), CHUNK)
        if mode == "hgrn":
            lb = lb_ref[...]
            f = lb + (1.0 - lb) * _sigmoid(zf_ref[rows, :])
            logf, k = jnp.log(f), 1.0 - f
            zq = zq_ref[rows, :]
            q = zq * _sigmoid(zq)
        else:
            ga_hi, ga_lo = _split_bf16(ga_ref[rows, :], 2)
            logits = (_dot(ga_hi, wg_hi_ref[...]) + _dot(ga_hi, wg_lo_ref[...])
                      + _dot(ga_lo, wg_hi_ref[...]) + bg_ref[...])
            logf = _log_sigmoid(logits) * (1.0 / GATE_NORM)
            k = zk_ref[rows, :]
            q = zq_ref[rows, :] * (dk ** -0.5)
        v = zv_ref[rows, :].astype(BF16)
        parts = _split_bf16(logf, 3)
        gall = _dot(sums, parts[0]) + _dot(sums, parts[1]) + _dot(sums, parts[2])
        g, gr, gl = gall[:CHUNK], gall[CHUNK:2 * CHUNK], gall[2 * CHUNK:]
        qg = (q * jnp.exp(g - gr)).astype(BF16)
        kg = (k * jnp.exp(gr - g)).astype(BF16)
        qs = q * jnp.exp(g)
        kd = k * jnp.exp(gl - g)
        el = jnp.exp(gl)
        gate = zg_ref[rows, :]
        outs = []
        for h in range(nh):
            ks, vs = slice(h * dkp, (h + 1) * dkp), slice(h * dvp, (h + 1) * dvp)
            a = jnp.where(causal, _dot_nt(qg[:, ks], kg[:, ks]), 0.0)
            o = _dot(a.astype(BF16), v[:, vs])
            for s in range(nsq):
                if nsq == 1:
                    qs_s, kd_s = qs[:, ks], kd[:, ks]
                else:
                    qs_s = jnp.where(row_seq == s, qs[:, ks], 0.0)
                    kd_s = jnp.where(row_seq == s, kd[:, ks], 0.0)
                st = s_ref[s, h]
                o = o + _dot(qs_s.astype(BF16), st.astype(BF16))
                e_col = jnp.broadcast_to(el[s * lq:s * lq + 1, ks], (dkp, dkp)).T
                if dvp > dkp:
                    e_col = jnp.concatenate([e_col] * (dvp // dkp), axis=1)
                s_ref[s, h] = e_col * st + _dot_tn(kd_s.astype(BF16), v[:, vs])
            nwh = nw_ref[...]
            gh = gate[:, vs]
            outs.append(_rms(o, nwh, dv) * (gh * _sigmoid(gh)))
        o_ref[rows, :] = jnp.concatenate(outs, axis=-1).astype(o_ref.dtype)

    if nblk == 1:
        chunk(0)
    else:
        def loop_body(ci, carry):
            chunk(ci)
            return carry
        lax.fori_loop(0, nblk, loop_body, 0)

    if has_s0:
        sout_ref[...] = s_ref[:, :, 0:dk, 0:dv]
    else:
        @pl.when(step == pl.num_programs(1) - 1)
        def _():
            sout_ref[...] = s_ref[:, :, 0:dk, 0:dv]


def _gated_linear(mode, arrays, col_blocks, params, s0, b, t, dk, dv, dkp, dvp):
    nh = 4
    m = b * t
    lq = min(CHUNK, t)
    nsq = CHUNK // lq
    if s0 is None:
        nblk = min(8, t // CHUNK)
        grid = (b, t // (CHUNK * nblk))
        ns = grid[1]
        row_map = lambda bi, si: bi * ns + si
        state_map = lambda bi, si: (bi, 0, 0, 0)
    else:
        nblk = 1
        grid = (m // CHUNK, 1)
        row_map = lambda bi, si: bi
        state_map = lambda bi, si: (bi, 0, 0, 0)
    tr = CHUNK * nblk
    in_specs = [pl.BlockSpec((tr, w), functools.partial(lambda bi, si, cb: (row_map(bi, si), cb), cb=cb))
                for (w, cb) in col_blocks]
    in_specs += [pl.BlockSpec(p.shape, lambda bi, si: (0, 0)) for p in params]
    operands = list(arrays) + list(params)
    if s0 is not None:
        in_specs.append(pl.BlockSpec((nsq, nh, dk, dv), state_map))
        operands.append(s0)
    o, s_out = pl.pallas_call(
        functools.partial(_gated_body, mode=mode, lq=lq, nblk=nblk, dk=dk, dv=dv, dkp=dkp, dvp=dvp,
                          has_s0=s0 is not None),
        grid=grid,
        in_specs=in_specs,
        out_specs=[pl.BlockSpec((tr, nh * dvp), lambda bi, si: (row_map(bi, si), 0)),
                   pl.BlockSpec((nsq, nh, dk, dv), state_map)],
        out_shape=[jax.ShapeDtypeStruct((m, nh * dvp), BF16),
                   jax.ShapeDtypeStruct((b, nh, dk, dv), F32)],
        scratch_shapes=[pltpu.VMEM((nsq, nh, dkp, dvp), F32)],
        compiler_params=_cparams("parallel", "arbitrary"),
        name="gated_linear_" + mode,
    )(*operands)
    return o, s_out


def _ffn_body(x_ref, oa_ref, ob_ref, woa_ref, wob_ref, fnw_ref, wg_ref, wu_ref, wo_ref, finw_ref,
              y_ref, h_ref, xn_ref, acc_ref, *, final_norm):
    j = pl.program_id(1)

    @pl.when(j == 0)
    def _():
        h = (x_ref[...] + _dot(oa_ref[...].astype(BF16), woa_ref[...])
             + _dot(ob_ref[...].astype(BF16), wob_ref[...]))
        h_ref[...] = h
        xn_ref[...] = _rms(h, fnw_ref[...]).astype(BF16)
        acc_ref[...] = jnp.zeros(acc_ref.shape, F32)

    xn = xn_ref[...]
    gate = _dot(xn, wg_ref[...])
    up = _dot(xn, wu_ref[...])
    act = (gate * _sigmoid(gate) * up).astype(BF16)
    acc_ref[...] += _dot(act, wo_ref[...])

    @pl.when(j == pl.num_programs(1) - 1)
    def _():
        y = h_ref[...] + acc_ref[...]
        if final_norm:
            y = _rms(y, finw_ref[...])
        y_ref[...] = y


def _mix_ffn(x, oa, ob, woa, wob, fnw, w_in, w_out, finw, final_norm):
    m = x.shape[0]
    tm = min(1024, m)
    tf = 256
    nf = D_FF // tf
    ka, kb = oa.shape[1], ob.shape[1]
    return pl.pallas_call(
        functools.partial(_ffn_body, final_norm=final_norm),
        grid=(m // tm, nf),
        in_specs=[pl.BlockSpec((tm, D_MODEL), lambda i, j: (i, 0)),
                  pl.BlockSpec((tm, ka), lambda i, j: (i, 0)),
                  pl.BlockSpec((tm, kb), lambda i, j: (i, 0)),
                  pl.BlockSpec((ka, D_MODEL), lambda i, j: (0, 0)),
                  pl.BlockSpec((kb, D_MODEL), lambda i, j: (0, 0)),
                  pl.BlockSpec((1, D_MODEL), lambda i, j: (0, 0)),
                  pl.BlockSpec((D_MODEL, tf), lambda i, j: (0, j)),
                  pl.BlockSpec((D_MODEL, tf), lambda i, j: (0, j + nf)),
                  pl.BlockSpec((tf, D_MODEL), lambda i, j: (j, 0)),
                  pl.BlockSpec((1, D_MODEL), lambda i, j: (0, 0))],
        out_specs=pl.BlockSpec((tm, D_MODEL), lambda i, j: (i, 0)),
        out_shape=jax.ShapeDtypeStruct((m, D_MODEL), F32),
        scratch_shapes=[pltpu.VMEM((tm, D_MODEL), F32),
                        pltpu.VMEM((tm, D_MODEL), BF16),
                        pltpu.VMEM((tm, D_MODEL), F32)],
        compiler_params=_cparams("parallel", "arbitrary"),
        name="mix_ffn",
    )(x, oa, ob, woa, wob, fnw.reshape(1, -1), w_in, w_in, w_out, finw.reshape(1, -1))


def _pad_heads(w, nh, d, dp, axis):
    shape = w.shape
    w = w.reshape(shape[:axis] + (nh, d) + shape[axis + 1:])
    pad = [(0, 0)] * w.ndim
    pad[axis + 1] = (0, dp - d)
    w = jnp.pad(w, pad)
    return w.reshape(shape[:axis] + (nh * dp,) + shape[axis + 1:])


def _rope_tables(pos):
    inv = ROPE_THETA ** (-jnp.arange(0, ROPE_A, 2, dtype=F32) / ROPE_A)
    ang = pos[:, None] * inv[None, :]
    cos = jnp.repeat(jnp.cos(ang), 2, axis=1)
    sin = jnp.repeat(jnp.sin(ang), 2, axis=1) * jnp.tile(jnp.array([-1.0, 1.0], F32), ROPE_A // 2)[None, :]
    return jnp.tile(cos, (1, H_A)), jnp.tile(sin, (1, H_A))


def _prepare_weights(P):
    W = {}
    w_ab = P["w_in_ab"][0]
    o_ckv, o_kpe, o_h = Q_LORA, Q_LORA + KV_LORA, Q_LORA + KV_LORA + ROPE_A
    W["w_mla"] = jnp.concatenate(
        [w_ab[:, :o_kpe], jnp.tile(w_ab[:, o_kpe:o_h], (1, H_A))], axis=1).astype(BF16)
    W["w_hgrn"] = w_ab[:, o_h:].astype(BF16)
    wuq = P["mla_w_uq"][0].reshape(Q_LORA, H_A, NOPE_A + ROPE_A)
    W["w_uq"] = jnp.concatenate([wuq[:, :, :NOPE_A].reshape(Q_LORA, -1),
                                 wuq[:, :, NOPE_A:].reshape(Q_LORA, -1)], axis=1).astype(BF16)
    half = (jnp.arange(H_A) % 2)[:, None, None]
    wk = jnp.transpose(P["mla_w_uk"][0], (1, 2, 0))
    W["w_uk"] = jnp.where(half == 0, jnp.pad(wk, ((0, 0), (0, NOPE_A), (0, 0))),
                          jnp.pad(wk, ((0, 0), (NOPE_A, 0), (0, 0)))).astype(BF16)
    wv = jnp.transpose(P["mla_w_uv"][0], (1, 0, 2))
    W["w_uv"] = jnp.where(half == 0, jnp.pad(wv, ((0, 0), (0, 0), (0, V_A))),
                          jnp.pad(wv, ((0, 0), (0, 0), (V_A, 0)))).astype(BF16)
    lb = jnp.cumsum(jax.nn.softmax(P["hgrn_lb_logits"].astype(F32), axis=0), axis=0)[0]
    W["hgrn_lb"] = lb.reshape(1, -1)
    W["hgrn_nw"] = P["hgrn_norm_w"][0].reshape(1, -1)
    w_out_ab = P["w_out_ab"][0].astype(BF16)
    W["w_out_a"], W["w_out_b"] = w_out_ab[:H_A * V_A], w_out_ab[H_A * V_A:]

    w_cd = P["w_in_cd"][0]
    offs = [0]
    for wdt in (H_C * DK_C, H_C * DK_C, H_C * DV_C, H_C * DV_C, GATE_RANK, H_D * DH_D * 3):
        offs.append(offs[-1] + wdt)
    seg = lambda i: w_cd[:, offs[i]:offs[i + 1]]
    W["w_gla"] = jnp.concatenate(
        [_pad_heads(seg(0), H_C, DK_C, DKP_C, 1), _pad_heads(seg(1), H_C, DK_C, DKP_C, 1),
         _pad_heads(seg(2), H_C, DV_C, DVP_C, 1), _pad_heads(seg(3), H_C, DV_C, DVP_C, 1),
         jnp.pad(seg(4), ((0, 0), (0, GATE_RANK_P - GATE_RANK)))], axis=1).astype(BF16)
    W["w_sb"] = seg(5).astype(BF16)
    wg2 = jnp.pad(_pad_heads(P["gla_w_gate2"][0], H_C, DK_C, DKP_C, 1), ((0, GATE_RANK_P - GATE_RANK), (0, 0)))
    W["wg2_hi"] = wg2.astype(BF16)
    W["wg2_lo"] = (wg2 - W["wg2_hi"].astype(F32)).astype(BF16)
    W["bg2"] = _pad_heads(P["gla_b_gate2"][0].reshape(1, -1), H_C, DK_C, DKP_C, 1)
    W["gla_nw"] = jnp.pad(P["gla_norm_w"][0], (0, DVP_C - DV_C)).reshape(1, -1)
    w_out_cd = P["w_out_cd"][0]
    W["w_out_c"] = _pad_heads(w_out_cd[:H_C * DV_C], H_C, DV_C, DVP_C, 0).astype(BF16)
    W["w_out_d"] = w_out_cd[H_C * DV_C:].astype(BF16)
    W["w_ffn_in"] = [P["w_ffn_in"][l].astype(BF16) for l in range(2)]
    W["w_ffn_out"] = [P["w_ffn_out"][l].astype(BF16) for l in range(2)]
    return W


def _trunk(x3, pos, P, W, ctx):
    b, t, _ = x3.shape
    m = b * t
    x = x3.reshape(m, D_MODEL)
    sample = ctx is not None
    cos, sin = _rope_tables(pos)
    reps = min(512, m) // t
    if reps > 1:
        cos, sin = jnp.tile(cos, (reps, 1)), jnp.tile(sin, (reps, 1))

    qlat, qpe, ckv, kpe, kpet = _mla_prep(
        x, P["attn_norm_w"][0], W["w_mla"], P["mla_q_norm_w"][0], W["w_uq"], W["w_uk"],
        P["mla_kv_norm_w"][0], cos, sin, F32 if sample else BF16)
    zh = _norm_matmul(x, P["attn_norm_w"][0], W["w_hgrn"], H_B * DK_B, split=True)
    if sample:
        o_a = _mla_sample_attn(ctx["page_table"], qlat, qpe, ckv, kpe, W["w_uv"],
                               ctx["cache_mla_ckv"], ctx["cache_mla_kpe"], t, ctx["pages_per_step"])
        s0 = ctx["state_hgrn"][0]
    else:
        o_a = _mla_prompt_attn(qlat, qpe, ckv, kpet, W["w_uv"], b, t)
        s0 = None
    wb = H_B * DK_B
    o_b, s_hgrn = _gated_linear("hgrn", zh, [(wb, 0)] * 4, [W["hgrn_lb"], W["hgrn_nw"]], s0,
                                b, t, DK_B, DV_B, DK_B, DV_B)
    h1 = _mix_ffn(x, o_a, o_b, W["w_out_a"], W["w_out_b"], P["ffn_norm_w"][0],
                  W["w_ffn_in"][0], W["w_ffn_out"][0], P["final_norm_w"], final_norm=False)

    zg = _norm_matmul(h1, P["attn_norm_w"][1], W["w_gla"], 640, split=False)
    sq, sk, sv = _norm_matmul(h1, P["attn_norm_w"][1], W["w_sb"], H_D * DH_D, split=True)
    if sample:
        o_d = _sb_sample_attn(ctx["page_table"], sq, sk, sv, ctx["cache_sb_k"], ctx["cache_sb_v"],
                              t, ctx["pages_per_step"])
        s0 = ctx["state_gla"][0]
    else:
        o_d = _sb_prompt_attn(sq, sk, sv, b, t)
        s0 = None
    wq, wv = H_C * DKP_C, H_C * DVP_C
    o_c, s_gla = _gated_linear(
        "gla", [zg] * 5, [(wq, 0), (wq, 1), (wv, 1), (wv, 2), (GATE_RANK_P, (2 * wq + 2 * wv) // GATE_RANK_P)],
        [W["wg2_hi"], W["wg2_lo"], W["bg2"], W["gla_nw"]], s0, b, t, DK_C, DV_C, DKP_C, DVP_C)
    y = _mix_ffn(h1, o_c, o_d, W["w_out_c"], W["w_out_d"], P["ffn_norm_w"][1],
                 W["w_ffn_in"][1], W["w_ffn_out"][1], P["final_norm_w"], final_norm=True)

    new = (ckv.reshape(1, b, t, KV_LORA), kpe.reshape(1, b, t, ROPE_A), s_hgrn[None],
           sk.reshape(1, b, t, H_D, DH_D), sv.reshape(1, b, t, H_D, DH_D), s_gla[None])
    return y.reshape(b, t, D_MODEL), new


def kernel(x_prompt, x_sample, cache_mla_ckv, cache_mla_kpe, state_hgrn, cache_sb_k, cache_sb_v, state_gla, page_table, attn_norm_w, ffn_norm_w, final_norm_w, w_in_ab, mla_q_norm_w, mla_w_uq, mla_kv_norm_w, mla_w_uk, mla_w_uv, hgrn_lb_logits, hgrn_norm_w, w_out_ab, w_in_cd, gla_w_gate2, gla_b_gate2, gla_norm_w, w_out_cd, w_ffn_in, w_ffn_out):
    P = {
        "attn_norm_w": attn_norm_w, "ffn_norm_w": ffn_norm_w, "final_norm_w": final_norm_w,
        "w_in_ab": w_in_ab, "mla_q_norm_w": mla_q_norm_w, "mla_w_uq": mla_w_uq,
        "mla_kv_norm_w": mla_kv_norm_w, "mla_w_uk": mla_w_uk, "mla_w_uv": mla_w_uv,
        "hgrn_lb_logits": hgrn_lb_logits, "hgrn_norm_w": hgrn_norm_w, "w_out_ab": w_out_ab,
        "w_in_cd": w_in_cd, "gla_w_gate2": gla_w_gate2, "gla_b_gate2": gla_b_gate2,
        "gla_norm_w": gla_norm_w, "w_out_cd": w_out_cd, "w_ffn_in": w_ffn_in, "w_ffn_out": w_ffn_out,
    }
    W = _prepare_weights(P)
    n_pages = page_table.shape[1]
    past = n_pages * PAGE
    pos_prompt = jnp.arange(x_prompt.shape[1], dtype=F32)
    pos_sample = jnp.arange(x_sample.shape[1], dtype=F32) + float(past)
    n_pool = cache_sb_k.shape[1]
    ctx = {
        "page_table": page_table,
        "cache_mla_ckv": cache_mla_ckv[0].reshape(n_pool, PAGE, KV_LORA),
        "cache_mla_kpe": cache_mla_kpe[0].reshape(n_pool, PAGE, ROPE_A),
        "state_hgrn": state_hgrn,
        "cache_sb_k": cache_sb_k[0].reshape(n_pool, PAGE, H_D * DH_D),
        "cache_sb_v": cache_sb_v[0].reshape(n_pool, PAGE, H_D * DH_D),
        "state_gla": state_gla,
        "pages_per_step": min(8, n_pages),
    }
    y_p, new_p = _trunk(x_prompt, pos_prompt, P, W, None)
    y_s, new_s = _trunk(x_sample, pos_sample, P, W, ctx)
    return (y_p, y_s) + new_p + new_s
```

```python
import functools

import jax
import jax.numpy as jnp
from jax import lax
from jax.experimental import pallas as pl
from jax.experimental.pallas import tpu as pltpu

F32 = jnp.float32
BF16 = jnp.bfloat16

D_MODEL = 1024
PAGE = 128
H_A, NOPE_A, ROPE_A, V_A = 8, 64, 32, 64
Q_LORA, KV_LORA = 384, 256
ROPE_THETA = 10000.0
MLA_SCALE = (NOPE_A + ROPE_A) ** -0.5
H_B, DK_B, DV_B = 4, 128, 128
H_C, DK_C, DV_C = 4, 96, 192
DKP_C, DVP_C = 128, 256
GATE_RANK, GATE_RANK_P = 16, 128
GATE_NORM = 16.0
H_D, DH_D = 4, 64
SB_SCALE = DH_D ** -0.5
D_FF = 2816
EPS = 1e-6
CHUNK = 64
ROPE_W = H_A * ROPE_A

VMEM_LIMIT = 56 * 1024 * 1024


def _cparams(*sem):
    return pltpu.CompilerParams(dimension_semantics=sem, vmem_limit_bytes=VMEM_LIMIT)


def _dot(a, b):
    return jnp.dot(a, b, preferred_element_type=F32)


def _dot_nt(a, b):
    return lax.dot_general(a, b, (((1,), (1,)), ((), ())), preferred_element_type=F32)


def _dot_tn(a, b):
    return lax.dot_general(a, b, (((0,), (0,)), ((), ())), preferred_element_type=F32)


def _split_bf16(x, n):
    parts = []
    for _ in range(n - 1):
        p = x.astype(BF16)
        parts.append(p)
        x = x - p.astype(F32)
    parts.append(x.astype(BF16))
    return parts


def _rms(x, w, n=None):
    n = x.shape[-1] if n is None else n
    ms = jnp.sum(x * x, axis=-1, keepdims=True) * (1.0 / n)
    return x * lax.rsqrt(ms + EPS) * w


def _sigmoid(x):
    return 1.0 / (1.0 + jnp.exp(-x))


def _log_sigmoid(x):
    return jnp.minimum(x, 0.0) - jnp.log(1.0 + jnp.exp(-jnp.abs(x)))


def _rope(x, cos, sin_signed):
    n = x.shape[-1]
    lane = lax.broadcasted_iota(jnp.int32, x.shape, x.ndim - 1)
    nxt = pltpu.roll(x, n - 1, x.ndim - 1)
    prv = pltpu.roll(x, 1, x.ndim - 1)
    swap = jnp.where((lane & 1) == 0, nxt, prv)
    return x * cos + swap * sin_signed


def _norm_matmul_body(x_ref, nw_ref, w_ref, *rest, n_out):
    outs, xn_ref = rest[:n_out], rest[n_out]
    j = pl.program_id(1)

    @pl.when(j == 0)
    def _():
        xn_ref[...] = _rms(x_ref[...], nw_ref[...]).astype(BF16)

    z = _dot(xn_ref[...], w_ref[...])
    if n_out == 1:
        outs[0][...] = z
    else:
        for k in range(n_out):
            @pl.when(j == k)
            def _(k=k):
                outs[k][...] = z


def _norm_matmul(x, nw, w, tn, split):
    m, kdim = x.shape
    n = w.shape[1]
    tm = min(512, m)
    nj = n // tn
    n_out = nj if split else 1
    if split:
        out_shape = [jax.ShapeDtypeStruct((m, tn), F32)] * nj
        out_specs = [pl.BlockSpec((tm, tn), lambda i, j: (i, 0))] * nj
    else:
        out_shape = [jax.ShapeDtypeStruct((m, n), F32)]
        out_specs = [pl.BlockSpec((tm, tn), lambda i, j: (i, j))]
    outs = pl.pallas_call(
        functools.partial(_norm_matmul_body, n_out=n_out),
        grid=(m // tm, nj),
        in_specs=[pl.BlockSpec((tm, kdim), lambda i, j: (i, 0)),
                  pl.BlockSpec((1, kdim), lambda i, j: (0, 0)),
                  pl.BlockSpec((kdim, tn), lambda i, j: (0, j))],
        out_specs=out_specs,
        out_shape=out_shape,
        scratch_shapes=[pltpu.VMEM((tm, kdim), BF16)],
        compiler_params=_cparams("parallel", "arbitrary"),
        name="norm_matmul",
    )(x, nw.reshape(1, kdim), w)
    return outs if split else outs[0]


def _mla_prep_body(x_ref, anw_ref, w_ref, qnw_ref, wuq_ref, wk_ref, kvnw_ref, cos_ref, sin_ref,
                   qlat_ref, qpe_ref, ckv_ref, kpe_ref, kpet_ref):
    xn = _rms(x_ref[...], anw_ref[...]).astype(BF16)
    z = _dot(xn, w_ref[...])
    cqn = _rms(z[:, :Q_LORA], qnw_ref[...]).astype(BF16)
    q = _dot(cqn, wuq_ref[...])
    qn = q[:, :H_A * NOPE_A].astype(BF16)
    for h in range(H_A):
        p = h // 2
        ql = _dot(qn[:, 128 * p:128 * (p + 1)], wk_ref[h])
        qlat_ref[h] = (ql * MLA_SCALE).astype(qlat_ref.dtype)
    cos, sin = cos_ref[...], sin_ref[...]
    qr = _rope(q[:, H_A * NOPE_A:], cos, sin) * MLA_SCALE
    qpe_ref[...] = qr.astype(qpe_ref.dtype)
    ckv_ref[...] = _rms(z[:, Q_LORA:Q_LORA + KV_LORA], kvnw_ref[...])
    kr = _rope(z[:, Q_LORA + KV_LORA:], cos, sin)
    kpet_ref[...] = kr.astype(BF16)
    kpe_ref[...] = kr[:, :ROPE_A]


def _mla_prep(x, anw, w_mla, qnw, wuq, wk, kvnw, cos, sin, q_dtype):
    m = x.shape[0]
    tm = min(512, m)
    nt = cos.shape[0] // tm
    wz = w_mla.shape[1]
    return pl.pallas_call(
        _mla_prep_body,
        grid=(m // tm,),
        in_specs=[pl.BlockSpec((tm, D_MODEL), lambda i: (i, 0)),
                  pl.BlockSpec((1, D_MODEL), lambda i: (0, 0)),
                  pl.BlockSpec((D_MODEL, wz), lambda i: (0, 0)),
                  pl.BlockSpec((1, Q_LORA), lambda i: (0, 0)),
                  pl.BlockSpec(wuq.shape, lambda i: (0, 0)),
                  pl.BlockSpec(wk.shape, lambda i: (0, 0, 0)),
                  pl.BlockSpec((1, KV_LORA), lambda i: (0, 0)),
                  pl.BlockSpec((tm, ROPE_W), lambda i: (i % nt, 0)),
                  pl.BlockSpec((tm, ROPE_W), lambda i: (i % nt, 0))],
        out_specs=[pl.BlockSpec((H_A, tm, KV_LORA), lambda i: (0, i, 0)),
                   pl.BlockSpec((tm, ROPE_W), lambda i: (i, 0)),
                   pl.BlockSpec((tm, KV_LORA), lambda i: (i, 0)),
                   pl.BlockSpec((tm, ROPE_A), lambda i: (i, 0)),
                   pl.BlockSpec((tm, ROPE_W), lambda i: (i, 0))],
        out_shape=[jax.ShapeDtypeStruct((H_A, m, KV_LORA), q_dtype),
                   jax.ShapeDtypeStruct((m, ROPE_W), q_dtype),
                   jax.ShapeDtypeStruct((m, KV_LORA), F32),
                   jax.ShapeDtypeStruct((m, ROPE_A), F32),
                   jax.ShapeDtypeStruct((m, ROPE_W), BF16)],
        compiler_params=_cparams("parallel"),
        name="mla_prep",
    )(x, anw.reshape(1, -1), w_mla, qnw.reshape(1, -1), wuq, wk, kvnw.reshape(1, -1), cos, sin)


def _head_stack(x, width, n_heads):
    lane = lax.broadcasted_iota(jnp.int32, x.shape, 1)
    zero = jnp.zeros_like(x)
    return jnp.concatenate(
        [jnp.where((lane >= h * width) & (lane < (h + 1) * width), x, zero) for h in range(n_heads)], axis=0)


def _uv_project(o, wuv_ref, t):
    ob = o.astype(BF16)
    pieces = []
    for p in range(H_A // 2):
        pieces.append(_dot(ob[2 * p * t:(2 * p + 1) * t], wuv_ref[2 * p])
                      + _dot(ob[(2 * p + 1) * t:(2 * p + 2) * t], wuv_ref[2 * p + 1]))
    return jnp.concatenate(pieces, axis=-1)


def _mla_prompt_body(qlat_ref, qpe_ref, ckv_ref, kpet_ref, wuv_ref, o_ref,
                     ckb_ref, m_ref, l_ref, acc_ref, *, tq, tk):
    qi = pl.program_id(1)
    rows = H_A * tq

    @pl.when(qi == 0)
    def _():
        ckb_ref[...] = ckv_ref[...].astype(BF16)

    ql = qlat_ref[...].reshape(rows, KV_LORA)
    qp = _head_stack(qpe_ref[...], ROPE_A, H_A)
    m_ref[...] = jnp.full(m_ref.shape, -jnp.inf, F32)
    l_ref[...] = jnp.zeros(l_ref.shape, F32)
    acc_ref[...] = jnp.zeros(acc_ref.shape, F32)

    def block(k0, masked):
        ck = ckb_ref[pl.ds(k0, tk), :]
        kp = kpet_ref[pl.ds(k0, tk), :]
        s = _dot_nt(ql, ck) + _dot_nt(qp, kp)
        if masked:
            row = lax.broadcasted_iota(jnp.int32, s.shape, 0) & (tq - 1)
            col = lax.broadcasted_iota(jnp.int32, s.shape, 1)
            s = jnp.where(col + k0 <= row + qi * tq, s, -jnp.inf)
        m_prev = m_ref[...]
        m_new = jnp.maximum(m_prev, jnp.max(s, axis=-1, keepdims=True))
        alpha = jnp.exp(m_prev - m_new)
        p = jnp.exp(s - m_new)
        l_ref[...] = alpha * l_ref[...] + jnp.sum(p, axis=-1, keepdims=True)
        acc_ref[...] = alpha * acc_ref[...] + _dot(p.astype(BF16), ck)
        m_ref[...] = m_new

    nfull = (qi * tq) // tk

    def full_block(kb, carry):
        block(pl.multiple_of(kb * tk, tk), False)
        return carry

    lax.fori_loop(0, nfull, full_block, 0)
    block(pl.multiple_of(nfull * tk, tk), True)
    o = acc_ref[...] / l_ref[...]
    o_ref[...] = _uv_project(o, wuv_ref, tq).astype(o_ref.dtype)


def _mla_prompt_attn(qlat, qpe, ckv, kpet, wuv, b, t):
    tq = min(128, t)
    tk = min(256, t)
    nq = t // tq
    rows = H_A * tq
    return pl.pallas_call(
        functools.partial(_mla_prompt_body, tq=tq, tk=tk),
        grid=(b, nq),
        in_specs=[pl.BlockSpec((H_A, tq, KV_LORA), lambda bi, qi: (0, bi * nq + qi, 0)),
                  pl.BlockSpec((tq, ROPE_W), lambda bi, qi: (bi * nq + qi, 0)),
                  pl.BlockSpec((t, KV_LORA), lambda bi, qi: (bi, 0)),
                  pl.BlockSpec((t, ROPE_W), lambda bi, qi: (bi, 0)),
                  pl.BlockSpec(wuv.shape, lambda bi, qi: (0, 0, 0))],
        out_specs=pl.BlockSpec((tq, H_A * V_A), lambda bi, qi: (bi * nq + qi, 0)),
        out_shape=jax.ShapeDtypeStruct((b * t, H_A * V_A), BF16),
        scratch_shapes=[pltpu.VMEM((t, KV_LORA), BF16),
                        pltpu.VMEM((rows, 1), F32),
                        pltpu.VMEM((rows, 1), F32),
                        pltpu.VMEM((rows, KV_LORA), F32)],
        compiler_params=_cparams("parallel", "arbitrary"),
        name="mla_prompt_attn",
    )(qlat, qpe, ckv, kpet, wuv)


def _mla_sample_body(pt_ref, qlat_ref, qpe_ref, ckn_ref, kpn_ref, wuv_ref, *rest, tn, g):
    ck_refs, kp_refs = rest[:g], rest[g:2 * g]
    o_ref, ql_ref, qp_ref, ckpad_ref, kppad_ref, m_ref, l_ref, acc_ref = rest[2 * g:]
    step = pl.program_id(1)
    rows = H_A * tn

    def update(s, ck, carry):
        m_prev, l_prev, acc = carry
        m_new = jnp.maximum(m_prev, jnp.max(s, axis=-1, keepdims=True))
        alpha = jnp.exp(m_prev - m_new)
        p = jnp.exp(s - m_new)
        l_new = alpha * l_prev + jnp.sum(p, axis=-1, keepdims=True)
        return m_new, l_new, alpha * acc + _dot(p.astype(BF16), ck)

    @pl.when(step == 0)
    def _():
        ql_ref[...] = qlat_ref[...].reshape(rows, KV_LORA).astype(BF16)
        qps = _head_stack(qpe_ref[...], ROPE_A, H_A).astype(BF16)
        fr = lax.broadcasted_iota(jnp.int32, (ROPE_W, ROPE_A), 0)
        fc = lax.broadcasted_iota(jnp.int32, (ROPE_W, ROPE_A), 1)
        fold = jnp.where((fr & (ROPE_A - 1)) == fc, 1.0, 0.0).astype(BF16)
        qp_ref[...] = _dot(qps, fold).astype(BF16)
        ckpad_ref[...] = jnp.zeros(ckpad_ref.shape, F32)
        kppad_ref[...] = jnp.zeros(kppad_ref.shape, F32)
        ckpad_ref[0:tn, :] = ckn_ref[...]
        kppad_ref[0:tn, :] = kpn_ref[...]
        row = lax.broadcasted_iota(jnp.int32, (rows, PAGE), 0) & (tn - 1)
        col = lax.broadcasted_iota(jnp.int32, (rows, PAGE), 1)
        ck = ckpad_ref[...].astype(BF16)
        s = _dot_nt(ql_ref[...], ck) + _dot_nt(qp_ref[...], kppad_ref[...].astype(BF16))
        s = jnp.where(col <= row, s, -jnp.inf)
        init = (jnp.full((rows, 1), -jnp.inf, F32), jnp.zeros((rows, 1), F32),
                jnp.zeros((rows, KV_LORA), F32))
        m_ref[...], l_ref[...], acc_ref[...] = update(s, ck, init)

    ck = jnp.concatenate([r[...] for r in ck_refs], axis=0).astype(BF16)
    kp = jnp.concatenate([r[...] for r in kp_refs], axis=1).astype(BF16)
    s = _dot_nt(ql_ref[...], ck) + _dot(qp_ref[...], kp)
    m_ref[...], l_ref[...], acc_ref[...] = update(s, ck, (m_ref[...], l_ref[...], acc_ref[...]))

    @pl.when(step == pl.num_programs(1) - 1)
    def _():
        o = acc_ref[...] / l_ref[...]
        o_ref[...] = _uv_project(o, wuv_ref, tn).astype(o_ref.dtype)


def _mla_sample_attn(page_table, qlat, qpe, ckn, kpn, wuv, cache_ckv, cache_kpe, tn, g):
    bs, n_pages = page_table.shape
    rows = H_A * tn
    pt = page_table.reshape(-1)

    def page_map(i):
        return lambda b, s, pt_ref: (pt_ref[b * n_pages + s * g + i], 0, 0)

    in_specs = [pl.BlockSpec((H_A, tn, KV_LORA), lambda b, s, pt_ref: (0, b, 0)),
                pl.BlockSpec((tn, ROPE_W), lambda b, s, pt_ref: (b, 0)),
                pl.BlockSpec((tn, KV_LORA), lambda b, s, pt_ref: (b, 0)),
                pl.BlockSpec((tn, ROPE_A), lambda b, s, pt_ref: (b, 0)),
                pl.BlockSpec(wuv.shape, lambda b, s, pt_ref: (0, 0, 0))]
    in_specs += [pl.BlockSpec((None, PAGE, KV_LORA), page_map(i)) for i in range(g)]
    in_specs += [pl.BlockSpec((None, ROPE_A, PAGE), page_map(i)) for i in range(g)]
    return pl.pallas_call(
        functools.partial(_mla_sample_body, tn=tn, g=g),
        grid_spec=pltpu.PrefetchScalarGridSpec(
            num_scalar_prefetch=1,
            grid=(bs, n_pages // g),
            in_specs=in_specs,
            out_specs=pl.BlockSpec((tn, H_A * V_A), lambda b, s, pt_ref: (b, 0)),
            scratch_shapes=[pltpu.VMEM((rows, KV_LORA), BF16),
                            pltpu.VMEM((rows, ROPE_A), BF16),
                            pltpu.VMEM((PAGE, KV_LORA), F32),
                            pltpu.VMEM((PAGE, ROPE_A), F32),
                            pltpu.VMEM((rows, 1), F32),
                            pltpu.VMEM((rows, 1), F32),
                            pltpu.VMEM((rows, KV_LORA), F32)]),
        out_shape=jax.ShapeDtypeStruct((bs * tn, H_A * V_A), F32),
        compiler_params=_cparams("parallel", "arbitrary"),
        name="mla_sample_attn",
    )(pt, qlat, qpe, ckn, kpn, wuv, *([cache_ckv] * g), *([cache_kpe] * g))


def _sb_block(qs, k, v, rest, acc, upper, mask=None):
    z = _dot_nt(qs, k)
    l_neg = _log_sigmoid(-z)
    if mask is not None:
        l_neg = jnp.where(mask, l_neg, 0.0)
    hi, lo = _split_bf16(l_neg, 2)
    later = _dot(hi, upper) + _dot(lo, upper) + rest
    a = jnp.exp(z + l_neg + later)
    if mask is not None:
        a = jnp.where(mask, a, 0.0)
    acc = acc + _dot(a.astype(BF16), v)
    rest = rest + jnp.sum(l_neg, axis=-1, keepdims=True)
    return rest, acc


def _upper_ones(n):
    r = lax.broadcasted_iota(jnp.int32, (n, n), 0)
    c = lax.broadcasted_iota(jnp.int32, (n, n), 1)
    return jnp.where(r > c, 1.0, 0.0).astype(BF16)


def _head_collapse(acc, t, width, n_heads):
    lane = lax.broadcasted_iota(jnp.int32, (t, n_heads * width), 1)
    out = jnp.zeros((t, n_heads * width), F32)
    for h in range(n_heads):
        out = out + jnp.where((lane >= h * width) & (lane < (h + 1) * width), acc[h * t:(h + 1) * t], 0.0)
    return out


def _sb_prompt_body(q_ref, k_ref, v_ref, o_ref, kb_ref, vb_ref, rest_ref, acc_ref, *, tq, tk):
    qi = pl.program_id(1)
    rows = H_D * tq

    @pl.when(qi == 0)
    def _():
        kb_ref[...] = k_ref[...].astype(BF16)
        vb_ref[...] = v_ref[...].astype(BF16)

    qs = _head_stack(q_ref[...] * SB_SCALE, DH_D, H_D).astype(BF16)
    upper = _upper_ones(tk)
    rest_ref[...] = jnp.zeros(rest_ref.shape, F32)
    acc_ref[...] = jnp.zeros(acc_ref.shape, F32)

    def block(k0, masked):
        mask = None
        if masked:
            row = lax.broadcasted_iota(jnp.int32, (rows, tk), 0) & (tq - 1)
            col = lax.broadcasted_iota(jnp.int32, (rows, tk), 1)
            mask = col + k0 < row + qi * tq
        rest, acc = _sb_block(qs, kb_ref[pl.ds(k0, tk), :], vb_ref[pl.ds(k0, tk), :],
                              rest_ref[...], acc_ref[...], upper, mask)
        rest_ref[...] = rest
        acc_ref[...] = acc

    nfull = (qi * tq) // tk
    block(pl.multiple_of(nfull * tk, tk), True)

    def full_block(i, carry):
        block(pl.multiple_of((nfull - 1 - i) * tk, tk), False)
        return carry

    lax.fori_loop(0, nfull, full_block, 0)
    o_ref[...] = _head_collapse(acc_ref[...], tq, DH_D, H_D).astype(o_ref.dtype)


def _sb_prompt_attn(q, k, v, b, t):
    tq = min(128, t)
    tk = min(256, t)
    nq = t // tq
    rows = H_D * tq
    w = H_D * DH_D
    return pl.pallas_call(
        functools.partial(_sb_prompt_body, tq=tq, tk=tk),
        grid=(b, nq),
        in_specs=[pl.BlockSpec((tq, w), lambda bi, qi: (bi * nq + qi, 0)),
                  pl.BlockSpec((t, w), lambda bi, qi: (bi, 0)),
                  pl.BlockSpec((t, w), lambda bi, qi: (bi, 0))],
        out_specs=pl.BlockSpec((tq, w), lambda bi, qi: (bi * nq + qi, 0)),
        out_shape=jax.ShapeDtypeStruct((b * t, w), BF16),
        scratch_shapes=[pltpu.VMEM((t, w), BF16),
                        pltpu.VMEM((t, w), BF16),
                        pltpu.VMEM((rows, 1), F32),
                        pltpu.VMEM((rows, w), F32)],
        compiler_params=_cparams("parallel", "arbitrary"),
        name="sb_prompt_attn",
    )(q, k, v)


def _sb_sample_body(pt_ref, q_ref, kn_ref, vn_ref, *rest, tn, g):
    k_refs, v_refs = rest[:g], rest[g:2 * g]
    o_ref, qs_ref, kpad_ref, vpad_ref, rest_ref, acc_ref = rest[2 * g:]
    step = pl.program_id(1)
    rows = H_D * tn
    r = lax.broadcasted_iota(jnp.int32, (PAGE, 2 * PAGE), 0)
    c = lax.broadcasted_iota(jnp.int32, (PAGE, 2 * PAGE), 1)
    sums = jnp.where((c >= PAGE) | (r > c), 1.0, 0.0).astype(BF16)

    def page_sums(l_negs):
        split = [_split_bf16(x, 2) for x in l_negs]
        stacked = jnp.concatenate([s[0] for s in split] + [s[1] for s in split], axis=0)
        cs = _dot(stacked, sums)
        n = len(l_negs) * rows
        cs = cs[:n] + cs[n:]
        return [(cs[i * rows:(i + 1) * rows, :PAGE], cs[i * rows:(i + 1) * rows, PAGE:])
                for i in range(len(l_negs))]

    @pl.when(step == 0)
    def _():
        qs_ref[...] = _head_stack(q_ref[...] * SB_SCALE, DH_D, H_D).astype(BF16)
        kpad_ref[...] = jnp.zeros(kpad_ref.shape, F32)
        vpad_ref[...] = jnp.zeros(vpad_ref.shape, F32)
        kpad_ref[0:tn, :] = kn_ref[...]
        vpad_ref[0:tn, :] = vn_ref[...]
        row = lax.broadcasted_iota(jnp.int32, (rows, PAGE), 0) & (tn - 1)
        col = lax.broadcasted_iota(jnp.int32, (rows, PAGE), 1)
        mask = col < row
        z = _dot_nt(qs_ref[...], kpad_ref[...].astype(BF16))
        l_neg = jnp.where(mask, _log_sigmoid(-z), 0.0)
        (later, total), = page_sums([l_neg])
        a = jnp.where(mask, jnp.exp(z + l_neg + later), 0.0)
        acc_ref[...] = _dot(a.astype(BF16), vpad_ref[...].astype(BF16))
        rest_ref[...] = total

    kt = jnp.concatenate([x[...] for x in k_refs], axis=1).astype(BF16)
    vt = jnp.concatenate([x[...] for x in v_refs], axis=1).astype(BF16)
    z = _dot(qs_ref[...], kt)
    l_neg = _log_sigmoid(-z)
    zl = z + l_neg
    stick = rest_ref[...]
    logits = []
    for i, (later, total) in enumerate(page_sums([l_neg[:, i * PAGE:(i + 1) * PAGE] for i in range(g)])):
        logits.append(zl[:, i * PAGE:(i + 1) * PAGE] + later + stick)
        stick = stick + total
    a = jnp.exp(jnp.concatenate(logits, axis=1)).astype(BF16)
    acc_ref[...] += _dot_nt(a, vt)
    rest_ref[...] = stick

    @pl.when(step == pl.num_programs(1) - 1)
    def _():
        o_ref[...] = _head_collapse(acc_ref[...], tn, DH_D, H_D).astype(o_ref.dtype)


def _sb_sample_attn(page_table, q, kn, vn, cache_k, cache_v, tn, g):
    bs, n_pages = page_table.shape
    rows = H_D * tn
    w = H_D * DH_D
    pt = page_table.reshape(-1)

    def page_map(i):
        return lambda b, s, pt_ref: (pt_ref[b * n_pages + n_pages - 1 - (s * g + i)], 0, 0)

    in_specs = [pl.BlockSpec((tn, w), lambda b, s, pt_ref: (b, 0))] * 3
    in_specs += [pl.BlockSpec((None, w, PAGE), page_map(i)) for i in range(g)] * 2
    return pl.pallas_call(
        functools.partial(_sb_sample_body, tn=tn, g=g),
        grid_spec=pltpu.PrefetchScalarGridSpec(
            num_scalar_prefetch=1,
            grid=(bs, n_pages // g),
            in_specs=in_specs,
            out_specs=pl.BlockSpec((tn, w), lambda b, s, pt_ref: (b, 0)),
            scratch_shapes=[pltpu.VMEM((rows, w), BF16),
                            pltpu.VMEM((PAGE, w), F32),
                            pltpu.VMEM((PAGE, w), F32),
                            pltpu.VMEM((rows, PAGE), F32),
                            pltpu.VMEM((rows, w), F32)]),
        out_shape=jax.ShapeDtypeStruct((bs * tn, w), F32),
        compiler_params=_cparams("parallel", "arbitrary"),
        name="sb_sample_attn",
    )(pt, q, kn, vn, *([cache_k] * g), *([cache_v] * g))


def _gated_body(*refs, mode, lq, nblk, dk, dv, dkp, dvp, has_s0):
    nh = 4
    nsq = CHUNK // lq
    it = iter(refs)
    if mode == "hgrn":
        zq_ref, zf_ref, zv_ref, zg_ref, lb_ref, nw_ref = (next(it) for _ in range(6))
    else:
        zq_ref, zk_ref, zv_ref, zg_ref, ga_ref, wg_hi_ref, wg_lo_ref, bg_ref, nw_ref = (next(it) for _ in range(9))
    s0_ref = next(it) if has_s0 else None
    o_ref, sout_ref, s_ref = next(it), next(it), next(it)
    step = pl.program_id(1)

    if has_s0:
        s_ref[...] = jnp.zeros(s_ref.shape, F32)
        s_ref[:, :, 0:dk, 0:dv] = s0_ref[...]
    else:
        @pl.when(step == 0)
        def _():
            s_ref[...] = jnp.zeros(s_ref.shape, F32)

    r = lax.broadcasted_iota(jnp.int32, (CHUNK, CHUNK), 0)
    c = lax.broadcasted_iota(jnp.int32, (CHUNK, CHUNK), 1)
    seq_r, seq_c = r // lq, c // lq
    same = seq_r == seq_c
    causal = same & (r >= c)
    ref_pos = seq_r * lq + lq // 2
    ones = lambda m: jnp.where(m, 1.0, 0.0).astype(BF16)
    sums = jnp.concatenate([ones(causal), ones(same & (c <= ref_pos)), ones(same)], axis=0)
    row_seq = lax.broadcasted_iota(jnp.int32, (CHUNK, dkp), 0) // lq

    def chunk(ci):
        rows = pl.ds(pl.multiple_of(ci * CHUNK, CHUNK), CHUNK)
        if mode == "hgrn":
            lb = lb_ref[...]
            f = lb + (1.0 - lb) * _sigmoid(zf_ref[rows, :])
            logf, k = jnp.log(f), 1.0 - f
            zq = zq_ref[rows, :]
            q = zq * _sigmoid(zq)
        else:
            ga_hi, ga_lo = _split_bf16(ga_ref[rows, :], 2)
            logits = (_dot(ga_hi, wg_hi_ref[...]) + _dot(ga_hi, wg_lo_ref[...])
                      + _dot(ga_lo, wg_hi_ref[...]) + bg_ref[...])
            logf = _log_sigmoid(logits) * (1.0 / GATE_NORM)
            k = zk_ref[rows, :]
            q = zq_ref[rows, :] * (dk ** -0.5)
        v = zv_ref[rows, :].astype(BF16)
        parts = _split_bf16(logf, 3)
        gall = _dot(sums, parts[0]) + _dot(sums, parts[1]) + _dot(sums, parts[2])
        g, gr, gl = gall[:CHUNK], gall[CHUNK:2 * CHUNK], gall[2 * CHUNK:]
        qg = (q * jnp.exp(g - gr)).astype(BF16)
        kg = (k * jnp.exp(gr - g)).astype(BF16)
        qs = q * jnp.exp(g)
        kd = k * jnp.exp(gl - g)
        el = jnp.exp(gl)
        gate = zg_ref[rows, :]
        outs = []
        for h in range(nh):
            ks, vs = slice(h * dkp, (h + 1) * dkp), slice(h * dvp, (h + 1) * dvp)
            a = jnp.where(causal, _dot_nt(qg[:, ks], kg[:, ks]), 0.0)
            o = _dot(a.astype(BF16), v[:, vs])
            for s in range(nsq):
                if nsq == 1:
                    qs_s, kd_s = qs[:, ks], kd[:, ks]
                else:
                    qs_s = jnp.where(row_seq == s, qs[:, ks], 0.0)
                    kd_s = jnp.where(row_seq == s, kd[:, ks], 0.0)
                st = s_ref[s, h]
                o = o + _dot(qs_s.astype(BF16), st.astype(BF16))
                e_col = jnp.broadcast_to(el[s * lq:s * lq + 1, ks], (dkp, dkp)).T
                if dvp > dkp:
                    e_col = jnp.concatenate([e_col] * (dvp // dkp), axis=1)
                s_ref[s, h] = e_col * st + _dot_tn(kd_s.astype(BF16), v[:, vs])
            nwh = nw_ref[...]
            gh = gate[:, vs]
            outs.append(_rms(o, nwh, dv) * (gh * _sigmoid(gh)))
        o_ref[rows, :] = jnp.concatenate(outs, axis=-1).astype(o_ref.dtype)

    if nblk == 1:
        chunk(0)
    else:
        def loop_body(ci, carry):
            chunk(ci)
            return carry
        lax.fori_loop(0, nblk, loop_body, 0)

    if has_s0:
        sout_ref[...] = s_ref[:, :, 0:dk, 0:dv]
    else:
        @pl.when(step == pl.num_programs(1) - 1)
        def _():
            sout_ref[...] = s_ref[:, :, 0:dk, 0:dv]


def _gated_linear(mode, arrays, col_blocks, params, s0, b, t, dk, dv, dkp, dvp):
    nh = 4
    m = b * t
    lq = min(CHUNK, t)
    nsq = CHUNK // lq
    if s0 is None:
        nblk = min(8, t // CHUNK)
        grid = (b, t // (CHUNK * nblk))
        ns = grid[1]
        row_map = lambda bi, si: bi * ns + si
        state_map = lambda bi, si: (bi, 0, 0, 0)
    else:
        nblk = 1
        grid = (m // CHUNK, 1)
        row_map = lambda bi, si: bi
        state_map = lambda bi, si: (bi, 0, 0, 0)
    tr = CHUNK * nblk
    in_specs = [pl.BlockSpec((tr, w), functools.partial(lambda bi, si, cb: (row_map(bi, si), cb), cb=cb))
                for (w, cb) in col_blocks]
    in_specs += [pl.BlockSpec(p.shape, lambda bi, si: (0, 0)) for p in params]
    operands = list(arrays) + list(params)
    if s0 is not None:
        in_specs.append(pl.BlockSpec((nsq, nh, dk, dv), state_map))
        operands.append(s0)
    o, s_out = pl.pallas_call(
        functools.partial(_gated_body, mode=mode, lq=lq, nblk=nblk, dk=dk, dv=dv, dkp=dkp, dvp=dvp,
                          has_s0=s0 is not None),
        grid=grid,
        in_specs=in_specs,
        out_specs=[pl.BlockSpec((tr, nh * dvp), lambda bi, si: (row_map(bi, si), 0)),
                   pl.BlockSpec((nsq, nh, dk, dv), state_map)],
        out_shape=[jax.ShapeDtypeStruct((m, nh * dvp), BF16),
                   jax.ShapeDtypeStruct((b, nh, dk, dv), F32)],
        scratch_shapes=[pltpu.VMEM((nsq, nh, dkp, dvp), F32)],
        compiler_params=_cparams("parallel", "arbitrary"),
        name="gated_linear_" + mode,
    )(*operands)
    return o, s_out


def _ffn_body(x_ref, oa_ref, ob_ref, woa_ref, wob_ref, fnw_ref, wg_ref, wu_ref, wo_ref, finw_ref,
              y_ref, h_ref, xn_ref, acc_ref, *, final_norm):
    j = pl.program_id(1)

    @pl.when(j == 0)
    def _():
        h = (x_ref[...] + _dot(oa_ref[...].astype(BF16), woa_ref[...])
             + _dot(ob_ref[...].astype(BF16), wob_ref[...]))
        h_ref[...] = h
        xn_ref[...] = _rms(h, fnw_ref[...]).astype(BF16)
        acc_ref[...] = jnp.zeros(acc_ref.shape, F32)

    xn = xn_ref[...]
    gate = _dot(xn, wg_ref[...])
    up = _dot(xn, wu_ref[...])
    act = (gate * _sigmoid(gate) * up).astype(BF16)
    acc_ref[...] += _dot(act, wo_ref[...])

    @pl.when(j == pl.num_programs(1) - 1)
    def _():
        y = h_ref[...] + acc_ref[...]
        if final_norm:
            y = _rms(y, finw_ref[...])
        y_ref[...] = y


def _mix_ffn(x, oa, ob, woa, wob, fnw, w_in, w_out, finw, final_norm):
    m = x.shape[0]
    tm = min(1024, m)
    tf = 256
    nf = D_FF // tf
    ka, kb = oa.shape[1], ob.shape[1]
    return pl.pallas_call(
        functools.partial(_ffn_body, final_norm=final_norm),
        grid=(m // tm, nf),
        in_specs=[pl.BlockSpec((tm, D_MODEL), lambda i, j: (i, 0)),
                  pl.BlockSpec((tm, ka), lambda i, j: (i, 0)),
                  pl.BlockSpec((tm, kb), lambda i, j: (i, 0)),
                  pl.BlockSpec((ka, D_MODEL), lambda i, j: (0, 0)),
                  pl.BlockSpec((kb, D_MODEL), lambda i, j: (0, 0)),
                  pl.BlockSpec((1, D_MODEL), lambda i, j: (0, 0)),
                  pl.BlockSpec((D_MODEL, tf), lambda i, j: (0, j)),
                  pl.BlockSpec((D_MODEL, tf), lambda i, j: (0, j + nf)),
                  pl.BlockSpec((tf, D_MODEL), lambda i, j: (j, 0)),
                  pl.BlockSpec((1, D_MODEL), lambda i, j: (0, 0))],
        out_specs=pl.BlockSpec((tm, D_MODEL), lambda i, j: (i, 0)),
        out_shape=jax.ShapeDtypeStruct((m, D_MODEL), F32),
        scratch_shapes=[pltpu.VMEM((tm, D_MODEL), F32),
                        pltpu.VMEM((tm, D_MODEL), BF16),
                        pltpu.VMEM((tm, D_MODEL), F32)],
        compiler_params=_cparams("parallel", "arbitrary"),
        name="mix_ffn",
    )(x, oa, ob, woa, wob, fnw.reshape(1, -1), w_in, w_in, w_out, finw.reshape(1, -1))


def _pad_heads(w, nh, d, dp, axis):
    shape = w.shape
    w = w.reshape(shape[:axis] + (nh, d) + shape[axis + 1:])
    pad = [(0, 0)] * w.ndim
    pad[axis + 1] = (0, dp - d)
    w = jnp.pad(w, pad)
    return w.reshape(shape[:axis] + (nh * dp,) + shape[axis + 1:])


def _rope_tables(pos):
    inv = ROPE_THETA ** (-jnp.arange(0, ROPE_A, 2, dtype=F32) / ROPE_A)
    ang = pos[:, None] * inv[None, :]
    cos = jnp.repeat(jnp.cos(ang), 2, axis=1)
    sin = jnp.repeat(jnp.sin(ang), 2, axis=1) * jnp.tile(jnp.array([-1.0, 1.0], F32), ROPE_A // 2)[None, :]
    return jnp.tile(cos, (1, H_A)), jnp.tile(sin, (1, H_A))


def _prepare_weights(P):
    W = {}
    w_ab = P["w_in_ab"][0]
    o_ckv, o_kpe, o_h = Q_LORA, Q_LORA + KV_LORA, Q_LORA + KV_LORA + ROPE_A
    W["w_mla"] = jnp.concatenate(
        [w_ab[:, :o_kpe], jnp.tile(w_ab[:, o_kpe:o_h], (1, H_A))], axis=1).astype(BF16)
    W["w_hgrn"] = w_ab[:, o_h:].astype(BF16)
    wuq = P["mla_w_uq"][0].reshape(Q_LORA, H_A, NOPE_A + ROPE_A)
    W["w_uq"] = jnp.concatenate([wuq[:, :, :NOPE_A].reshape(Q_LORA, -1),
                                 wuq[:, :, NOPE_A:].reshape(Q_LORA, -1)], axis=1).astype(BF16)
    half = (jnp.arange(H_A) % 2)[:, None, None]
    wk = jnp.transpose(P["mla_w_uk"][0], (1, 2, 0))
    W["w_uk"] = jnp.where(half == 0, jnp.pad(wk, ((0, 0), (0, NOPE_A), (0, 0))),
                          jnp.pad(wk, ((0, 0), (NOPE_A, 0), (0, 0)))).astype(BF16)
    wv = jnp.transpose(P["mla_w_uv"][0], (1, 0, 2))
    W["w_uv"] = jnp.where(half == 0, jnp.pad(wv, ((0, 0), (0, 0), (0, V_A))),
                          jnp.pad(wv, ((0, 0), (0, 0), (V_A, 0)))).astype(BF16)
    lb = jnp.cumsum(jax.nn.softmax(P["hgrn_lb_logits"].astype(F32), axis=0), axis=0)[0]
    W["hgrn_lb"] = lb.reshape(1, -1)
    W["hgrn_nw"] = P["hgrn_norm_w"][0].reshape(1, -1)
    w_out_ab = P["w_out_ab"][0].astype(BF16)
    W["w_out_a"], W["w_out_b"] = w_out_ab[:H_A * V_A], w_out_ab[H_A * V_A:]

    w_cd = P["w_in_cd"][0]
    offs = [0]
    for wdt in (H_C * DK_C, H_C * DK_C, H_C * DV_C, H_C * DV_C, GATE_RANK, H_D * DH_D * 3):
        offs.append(offs[-1] + wdt)
    seg = lambda i: w_cd[:, offs[i]:offs[i + 1]]
    W["w_gla"] = jnp.concatenate(
        [_pad_heads(seg(0), H_C, DK_C, DKP_C, 1), _pad_heads(seg(1), H_C, DK_C, DKP_C, 1),
         _pad_heads(seg(2), H_C, DV_C, DVP_C, 1), _pad_heads(seg(3), H_C, DV_C, DVP_C, 1),
         jnp.pad(seg(4), ((0, 0), (0, GATE_RANK_P - GATE_RANK)))], axis=1).astype(BF16)
    W["w_sb"] = seg(5).astype(BF16)
    wg2 = jnp.pad(_pad_heads(P["gla_w_gate2"][0], H_C, DK_C, DKP_C, 1), ((0, GATE_RANK_P - GATE_RANK), (0, 0)))
    W["wg2_hi"] = wg2.astype(BF16)
    W["wg2_lo"] = (wg2 - W["wg2_hi"].astype(F32)).astype(BF16)
    W["bg2"] = _pad_heads(P["gla_b_gate2"][0].reshape(1, -1), H_C, DK_C, DKP_C, 1)
    W["gla_nw"] = jnp.pad(P["gla_norm_w"][0], (0, DVP_C - DV_C)).reshape(1, -1)
    w_out_cd = P["w_out_cd"][0]
    W["w_out_c"] = _pad_heads(w_out_cd[:H_C * DV_C], H_C, DV_C, DVP_C, 0).astype(BF16)
    W["w_out_d"] = w_out_cd[H_C * DV_C:].astype(BF16)
    W["w_ffn_in"] = [P["w_ffn_in"][l].astype(BF16) for l in range(2)]
    W["w_ffn_out"] = [P["w_ffn_out"][l].astype(BF16) for l in range(2)]
    return W


def _trunk(x3, pos, P, W, ctx):
    b, t, _ = x3.shape
    m = b * t
    x = x3.reshape(m, D_MODEL)
    sample = ctx is not None
    cos, sin = _rope_tables(pos)
    reps = min(512, m) // t
    if reps > 1:
        cos, sin = jnp.tile(cos, (reps, 1)), jnp.tile(sin, (reps, 1))

    qlat, qpe, ckv, kpe, kpet = _mla_prep(
        x, P["attn_norm_w"][0], W["w_mla"], P["mla_q_norm_w"][0], W["w_uq"], W["w_uk"],
        P["mla_kv_norm_w"][0], cos, sin, F32 if sample else BF16)
    zh = _norm_matmul(x, P["attn_norm_w"][0], W["w_hgrn"], H_B * DK_B, split=True)
    if sample:
        o_a = _mla_sample_attn(ctx["page_table"], qlat, qpe, ckv, kpe, W["w_uv"],
                               ctx["cache_mla_ckv"], ctx["cache_mla_kpe"], t, ctx["pages_per_step"])
        s0 = ctx["state_hgrn"][0]
    else:
        o_a = _mla_prompt_attn(qlat, qpe, ckv, kpet, W["w_uv"], b, t)
        s0 = None
    wb = H_B * DK_B
    o_b, s_hgrn = _gated_linear("hgrn", zh, [(wb, 0)] * 4, [W["hgrn_lb"], W["hgrn_nw"]], s0,
                                b, t, DK_B, DV_B, DK_B, DV_B)
    h1 = _mix_ffn(x, o_a, o_b, W["w_out_a"], W["w_out_b"], P["ffn_norm_w"][0],
                  W["w_ffn_in"][0], W["w_ffn_out"][0], P["final_norm_w"], final_norm=False)

    zg = _norm_matmul(h1, P["attn_norm_w"][1], W["w_gla"], 640, split=False)
    sq, sk, sv = _norm_matmul(h1, P["attn_norm_w"][1], W["w_sb"], H_D * DH_D, split=True)
    if sample:
        o_d = _sb_sample_attn(ctx["page_table"], sq, sk, sv, ctx["cache_sb_k"], ctx["cache_sb_v"],
                              t, ctx["pages_per_step"])
        s0 = ctx["state_gla"][0]
    else:
        o_d = _sb_prompt_attn(sq, sk, sv, b, t)
        s0 = None
    wq, wv = H_C * DKP_C, H_C * DVP_C
    o_c, s_gla = _gated_linear(
        "gla", [zg] * 5, [(wq, 0), (wq, 1), (wv, 1), (wv, 2), (GATE_RANK_P, (2 * wq + 2 * wv) // GATE_RANK_P)],
        [W["wg2_hi"], W["wg2_lo"], W["bg2"], W["gla_nw"]], s0, b, t, DK_C, DV_C, DKP_C, DVP_C)
    y = _mix_ffn(h1, o_c, o_d, W["w_out_c"], W["w_out_d"], P["ffn_norm_w"][1],
                 W["w_ffn_in"][1], W["w_ffn_out"][1], P["final_norm_w"], final_norm=True)

    new = (ckv.reshape(1, b, t, KV_LORA), kpe.reshape(1, b, t, ROPE_A), s_hgrn[None],
           sk.reshape(1, b, t, H_D, DH_D), sv.reshape(1, b, t, H_D, DH_D), s_gla[None])
    return y.reshape(b, t, D_MODEL), new


def kernel(x_prompt, x_sample, cache_mla_ckv, cache_mla_kpe, state_hgrn, cache_sb_k, cache_sb_v, state_gla, page_table, attn_norm_w, ffn_norm_w, final_norm_w, w_in_ab, mla_q_norm_w, mla_w_uq, mla_kv_norm_w, mla_w_uk, mla_w_uv, hgrn_lb_logits, hgrn_norm_w, w_out_ab, w_in_cd, gla_w_gate2, gla_b_gate2, gla_norm_w, w_out_cd, w_ffn_in, w_ffn_out):
    P = {
        "attn_norm_w": attn_norm_w, "ffn_norm_w": ffn_norm_w, "final_norm_w": final_norm_w,
        "w_in_ab": w_in_ab, "mla_q_norm_w": mla_q_norm_w, "mla_w_uq": mla_w_uq,
        "mla_kv_norm_w": mla_kv_norm_w, "mla_w_uk": mla_w_uk, "mla_w_uv": mla_w_uv,
        "hgrn_lb_logits": hgrn_lb_logits, "hgrn_norm_w": hgrn_norm_w, "w_out_ab": w_out_ab,
        "w_in_cd": w_in_cd, "gla_w_gate2": gla_w_gate2, "gla_b_gate2": gla_b_gate2,
        "gla_norm_w": gla_norm_w, "w_out_cd": w_out_cd, "w_ffn_in": w_ffn_in, "w_ffn_out": w_ffn_out,
    }
    W = _prepare_weights(P)
    n_pages = page_table.shape[1]
    past = n_pages * PAGE
    pos_prompt = jnp.arange(x_prompt.shape[1], dtype=F32)
    pos_sample = jnp.arange(x_sample.shape[1], dtype=F32) + float(past)
    n_pool = cache_sb_k.shape[1]
    ctx = {
        "page_table": page_table,
        "cache_mla_ckv": cache_mla_ckv[0],
        "cache_mla_kpe": jnp.transpose(cache_mla_kpe[0], (0, 2, 1)),
        "state_hgrn": state_hgrn,
        "cache_sb_k": jnp.transpose(cache_sb_k[0], (0, 2, 3, 1)).reshape(n_pool, H_D * DH_D, PAGE),
        "cache_sb_v": jnp.transpose(cache_sb_v[0], (0, 2, 3, 1)).reshape(n_pool, H_D * DH_D, PAGE),
        "state_gla": state_gla,
        "pages_per_step": min(32, n_pages),
    }
    y_p, new_p = _trunk(x_prompt, pos_prompt, P, W, None)
    y_s, new_s = _trunk(x_sample, pos_sample, P, W, ctx)
    return (y_p, y_s) + new_p + new_s
```

```python
import functools

import jax
import jax.numpy as jnp
from jax import lax
from jax.experimental import pallas as pl
from jax.experimental.pallas import tpu as pltpu

F32 = jnp.float32
BF16 = jnp.bfloat16

D_MODEL = 1024
PAGE = 128
H_A, NOPE_A, ROPE_A, V_A = 8, 64, 32, 64
Q_LORA, KV_LORA = 384, 256
ROPE_THETA = 10000.0
MLA_SCALE = (NOPE_A + ROPE_A) ** -0.5
H_B, DK_B, DV_B = 4, 128, 128
H_C, DK_C, DV_C = 4, 96, 192
DKP_C, DVP_C = 128, 256
GATE_RANK, GATE_RANK_P = 16, 128
GATE_NORM = 16.0
H_D, DH_D = 4, 64
SB_SCALE = DH_D ** -0.5
D_FF = 2816
EPS = 1e-6
CHUNK = 64
ROPE_W = H_A * ROPE_A

VMEM_LIMIT = 56 * 1024 * 1024


def _cparams(*sem):
    return pltpu.CompilerParams(dimension_semantics=sem, vmem_limit_bytes=VMEM_LIMIT)


def _dot(a, b):
    return jnp.dot(a, b, preferred_element_type=F32)


def _dot_nt(a, b):
    return lax.dot_general(a, b, (((1,), (1,)), ((), ())), preferred_element_type=F32)


def _dot_tn(a, b):
    return lax.dot_general(a, b, (((0,), (0,)), ((), ())), preferred_element_type=F32)


def _split_bf16(x, n):
    parts = []
    for _ in range(n - 1):
        p = x.astype(BF16)
        parts.append(p)
        x = x - p.astype(F32)
    parts.append(x.astype(BF16))
    return parts


def _rms(x, w, n=None):
    n = x.shape[-1] if n is None else n
    ms = jnp.sum(x * x, axis=-1, keepdims=True) * (1.0 / n)
    return x * lax.rsqrt(ms + EPS) * w


def _sigmoid(x):
    return 1.0 / (1.0 + jnp.exp(-x))


def _log_sigmoid(x):
    return jnp.minimum(x, 0.0) - jnp.log(1.0 + jnp.exp(-jnp.abs(x)))


def _rope(x, cos, sin_signed):
    n = x.shape[-1]
    lane = lax.broadcasted_iota(jnp.int32, x.shape, x.ndim - 1)
    nxt = pltpu.roll(x, n - 1, x.ndim - 1)
    prv = pltpu.roll(x, 1, x.ndim - 1)
    swap = jnp.where((lane & 1) == 0, nxt, prv)
    return x * cos + swap * sin_signed


def _norm_matmul_body(x_ref, nw_ref, w_ref, *rest, n_out):
    outs, xn_ref = rest[:n_out], rest[n_out]
    j = pl.program_id(1)

    @pl.when(j == 0)
    def _():
        xn_ref[...] = _rms(x_ref[...], nw_ref[...]).astype(BF16)

    z = _dot(xn_ref[...], w_ref[...])
    if n_out == 1:
        outs[0][...] = z
    else:
        for k in range(n_out):
            @pl.when(j == k)
            def _(k=k):
                outs[k][...] = z


def _norm_matmul(x, nw, w, tn, split):
    m, kdim = x.shape
    n = w.shape[1]
    tm = min(1024, m)
    nj = n // tn
    n_out = nj if split else 1
    if split:
        out_shape = [jax.ShapeDtypeStruct((m, tn), F32)] * nj
        out_specs = [pl.BlockSpec((tm, tn), lambda i, j: (i, 0))] * nj
    else:
        out_shape = [jax.ShapeDtypeStruct((m, n), F32)]
        out_specs = [pl.BlockSpec((tm, tn), lambda i, j: (i, j))]
    outs = pl.pallas_call(
        functools.partial(_norm_matmul_body, n_out=n_out),
        grid=(m // tm, nj),
        in_specs=[pl.BlockSpec((tm, kdim), lambda i, j: (i, 0)),
                  pl.BlockSpec((1, kdim), lambda i, j: (0, 0)),
                  pl.BlockSpec((kdim, tn), lambda i, j: (0, j))],
        out_specs=out_specs,
        out_shape=out_shape,
        scratch_shapes=[pltpu.VMEM((tm, kdim), BF16)],
        compiler_params=_cparams("parallel", "arbitrary"),
        name="norm_matmul",
    )(x, nw.reshape(1, kdim), w)
    return outs if split else outs[0]


def _mla_prep_body(x_ref, anw_ref, w_ref, qnw_ref, wuq_ref, wk_ref, kvnw_ref, cos_ref, sin_ref,
                   qlat_ref, qpe_ref, ckv_ref, kpe_ref, kpet_ref, *, transposed_q):
    xn = _rms(x_ref[...], anw_ref[...]).astype(BF16)
    z = _dot(xn, w_ref[...])
    cqn = _rms(z[:, :Q_LORA], qnw_ref[...]).astype(BF16)
    q = _dot(cqn, wuq_ref[...])
    qn = q[:, :H_A * NOPE_A].astype(BF16)
    for h in range(H_A):
        qpair = qn[:, 128 * (h // 2):128 * (h // 2 + 1)]
        ql = _dot_nt(wk_ref[h], qpair) if transposed_q else _dot(qpair, wk_ref[h])
        qlat_ref[h] = (ql * MLA_SCALE).astype(qlat_ref.dtype)
    cos, sin = cos_ref[...], sin_ref[...]
    qr = _rope(q[:, H_A * NOPE_A:], cos, sin) * MLA_SCALE
    qpe_ref[...] = (qr.T if transposed_q else qr).astype(qpe_ref.dtype)
    ckv_ref[...] = _rms(z[:, Q_LORA:Q_LORA + KV_LORA], kvnw_ref[...])
    kr = _rope(z[:, Q_LORA + KV_LORA:], cos, sin)
    kpet_ref[...] = kr.astype(BF16)
    kpe_ref[...] = kr[:, :ROPE_A]


def _mla_prep(x, anw, w_mla, qnw, wuq, wk, kvnw, cos, sin, q_dtype, transposed_q):
    m = x.shape[0]
    tm = min(512, m)
    nt = cos.shape[0] // tm
    wz = w_mla.shape[1]
    if transposed_q:
        q_specs = [pl.BlockSpec((H_A, KV_LORA, tm), lambda i: (0, 0, i)),
                   pl.BlockSpec((ROPE_W, tm), lambda i: (0, i))]
        q_shapes = [jax.ShapeDtypeStruct((H_A, KV_LORA, m), q_dtype),
                    jax.ShapeDtypeStruct((ROPE_W, m), q_dtype)]
    else:
        q_specs = [pl.BlockSpec((H_A, tm, KV_LORA), lambda i: (0, i, 0)),
                   pl.BlockSpec((tm, ROPE_W), lambda i: (i, 0))]
        q_shapes = [jax.ShapeDtypeStruct((H_A, m, KV_LORA), q_dtype),
                    jax.ShapeDtypeStruct((m, ROPE_W), q_dtype)]
    return pl.pallas_call(
        functools.partial(_mla_prep_body, transposed_q=transposed_q),
        grid=(m // tm,),
        in_specs=[pl.BlockSpec((tm, D_MODEL), lambda i: (i, 0)),
                  pl.BlockSpec((1, D_MODEL), lambda i: (0, 0)),
                  pl.BlockSpec((D_MODEL, wz), lambda i: (0, 0)),
                  pl.BlockSpec((1, Q_LORA), lambda i: (0, 0)),
                  pl.BlockSpec(wuq.shape, lambda i: (0, 0)),
                  pl.BlockSpec(wk.shape, lambda i: (0, 0, 0)),
                  pl.BlockSpec((1, KV_LORA), lambda i: (0, 0)),
                  pl.BlockSpec((tm, ROPE_W), lambda i: (i % nt, 0)),
                  pl.BlockSpec((tm, ROPE_W), lambda i: (i % nt, 0))],
        out_specs=q_specs + [
                   pl.BlockSpec((tm, KV_LORA), lambda i: (i, 0)),
                   pl.BlockSpec((tm, ROPE_A), lambda i: (i, 0)),
                   pl.BlockSpec((tm, ROPE_W), lambda i: (i, 0))],
        out_shape=q_shapes + [
                   jax.ShapeDtypeStruct((m, KV_LORA), F32),
                   jax.ShapeDtypeStruct((m, ROPE_A), F32),
                   jax.ShapeDtypeStruct((m, ROPE_W), BF16)],
        compiler_params=_cparams("parallel"),
        name="mla_prep",
    )(x, anw.reshape(1, -1), w_mla, qnw.reshape(1, -1), wuq, wk, kvnw.reshape(1, -1), cos, sin)


def _head_stack(x, width, n_heads):
    lane = lax.broadcasted_iota(jnp.int32, x.shape, 1)
    zero = jnp.zeros_like(x)
    return jnp.concatenate(
        [jnp.where((lane >= h * width) & (lane < (h + 1) * width), x, zero) for h in range(n_heads)], axis=0)


def _uv_project(o, wuv_ref, t):
    ob = o.astype(BF16)
    pieces = []
    for p in range(H_A // 2):
        pieces.append(_dot(ob[2 * p * t:(2 * p + 1) * t], wuv_ref[2 * p])
                      + _dot(ob[(2 * p + 1) * t:(2 * p + 2) * t], wuv_ref[2 * p + 1]))
    return jnp.concatenate(pieces, axis=-1)


def _mla_prompt_body(qlat_ref, qpe_ref, ckv_ref, kpet_ref, wuv_ref, o_ref,
                     ckb_ref, ckt_ref, m_ref, l_ref, acc_ref, *, tq, tk):
    qi = pl.program_id(1)
    nkb = ckt_ref.shape[0]

    @pl.when(qi == 0)
    def _():
        ckb_ref[...] = ckv_ref[...].astype(BF16)
        for kb in range(nkb):
            ckt_ref[kb] = ckv_ref[kb * tk:(kb + 1) * tk, :].T.astype(BF16)

    ql = jnp.concatenate([qlat_ref[h] for h in range(H_A)], axis=1)
    qp = qpe_ref[...]
    feat = lax.broadcasted_iota(jnp.int32, qp.shape, 0)
    zero = jnp.zeros_like(qp)
    qp = jnp.concatenate([jnp.where((feat >= h * ROPE_A) & (feat < (h + 1) * ROPE_A), qp, zero)
                          for h in range(H_A)], axis=1)
    m_ref[...] = jnp.full(m_ref.shape, -jnp.inf, F32)
    l_ref[...] = jnp.zeros(l_ref.shape, F32)
    acc_ref[...] = jnp.zeros(acc_ref.shape, F32)

    def block(kb, masked):
        k0 = pl.multiple_of(kb * tk, tk)
        s = _dot(ckb_ref[pl.ds(k0, tk), :], ql) + _dot(kpet_ref[pl.ds(k0, tk), :], qp)
        if masked:
            key = lax.broadcasted_iota(jnp.int32, s.shape, 0) + k0
            tok = (lax.broadcasted_iota(jnp.int32, s.shape, 1) & (tq - 1)) + qi * tq
            s = jnp.where(key <= tok, s, -jnp.inf)
        m_prev = m_ref[...]
        m_new = jnp.maximum(m_prev, jnp.max(s, axis=0, keepdims=True))
        alpha = jnp.exp(m_prev - m_new)
        p = jnp.exp(s - m_new)
        l_ref[...] = alpha * l_ref[...] + jnp.sum(p, axis=0, keepdims=True)
        acc_ref[...] = alpha * acc_ref[...] + _dot(ckt_ref[kb], p.astype(BF16))
        m_ref[...] = m_new

    nfull = (qi * tq) // tk

    def full_block(kb, carry):
        block(kb, False)
        return carry

    lax.fori_loop(0, nfull, full_block, 0)
    block(nfull, True)
    ot = acc_ref[...] / l_ref[...]
    o = jnp.concatenate([ot[:, h * tq:(h + 1) * tq].T for h in range(H_A)], axis=0)
    o_ref[...] = _uv_project(o, wuv_ref, tq).astype(o_ref.dtype)


def _mla_prompt_attn(qlat, qpe, ckv, kpet, wuv, b, t):
    tq = min(128, t)
    tk = min(256, t)
    nq = t // tq
    cols = H_A * tq
    return pl.pallas_call(
        functools.partial(_mla_prompt_body, tq=tq, tk=tk),
        grid=(b, nq),
        in_specs=[pl.BlockSpec((H_A, KV_LORA, tq), lambda bi, qi: (0, 0, bi * nq + qi)),
                  pl.BlockSpec((ROPE_W, tq), lambda bi, qi: (0, bi * nq + qi)),
                  pl.BlockSpec((t, KV_LORA), lambda bi, qi: (bi, 0)),
                  pl.BlockSpec((t, ROPE_W), lambda bi, qi: (bi, 0)),
                  pl.BlockSpec(wuv.shape, lambda bi, qi: (0, 0, 0))],
        out_specs=pl.BlockSpec((tq, H_A * V_A), lambda bi, qi: (bi * nq + qi, 0)),
        out_shape=jax.ShapeDtypeStruct((b * t, H_A * V_A), BF16),
        scratch_shapes=[pltpu.VMEM((t, KV_LORA), BF16),
                        pltpu.VMEM((t // tk, KV_LORA, tk), BF16),
                        pltpu.VMEM((1, cols), F32),
                        pltpu.VMEM((1, cols), F32),
                        pltpu.VMEM((KV_LORA, cols), F32)],
        compiler_params=_cparams("parallel", "arbitrary"),
        name="mla_prompt_attn",
    )(qlat, qpe, ckv, kpet, wuv)


def _mla_sample_body(pt_ref, qlat_ref, qpe_ref, ckn_ref, kpn_ref, wuv_ref, *rest, tn, g, nchain):
    ck_refs, kp_refs = rest[:g], rest[g:2 * g]
    o_ref, ql_ref, qp_ref, ckpad_ref, kppad_ref, m_ref, l_ref, acc_ref = rest[2 * g:]
    step = pl.program_id(1)
    rows = H_A * tn

    def update(s, ck, carry):
        m_prev, l_prev, acc = carry
        m_new = jnp.maximum(m_prev, jnp.max(s, axis=-1, keepdims=True))
        alpha = jnp.exp(m_prev - m_new)
        p = jnp.exp(s - m_new)
        l_new = alpha * l_prev + jnp.sum(p, axis=-1, keepdims=True)
        return m_new, l_new, alpha * acc + _dot(p.astype(BF16), ck)

    @pl.when(step == 0)
    def _():
        ql_ref[...] = qlat_ref[...].reshape(rows, KV_LORA).astype(BF16)
        qps = _head_stack(qpe_ref[...], ROPE_A, H_A).astype(BF16)
        fr = lax.broadcasted_iota(jnp.int32, (ROPE_W, ROPE_A), 0)
        fc = lax.broadcasted_iota(jnp.int32, (ROPE_W, ROPE_A), 1)
        fold = jnp.where((fr & (ROPE_A - 1)) == fc, 1.0, 0.0).astype(BF16)
        qp_ref[...] = _dot(qps, fold).astype(BF16)
        ckpad_ref[...] = jnp.zeros(ckpad_ref.shape, F32)
        kppad_ref[...] = jnp.zeros(kppad_ref.shape, F32)
        ckpad_ref[0:tn, :] = ckn_ref[...]
        kppad_ref[0:tn, :] = kpn_ref[...]
        row = lax.broadcasted_iota(jnp.int32, (rows, PAGE), 0) & (tn - 1)
        col = lax.broadcasted_iota(jnp.int32, (rows, PAGE), 1)
        ck = ckpad_ref[...].astype(BF16)
        s = _dot_nt(ql_ref[...], ck) + _dot_nt(qp_ref[...], kppad_ref[...].astype(BF16))
        s = jnp.where(col <= row, s, -jnp.inf)
        init = (jnp.full((rows, 1), -jnp.inf, F32), jnp.zeros((rows, 1), F32),
                jnp.zeros((rows, KV_LORA), F32))
        m_ref[0], l_ref[0], acc_ref[0] = update(s, ck, init)
        for c in range(1, nchain):
            m_ref[c], l_ref[c], acc_ref[c] = init

    per = g // nchain
    for c in range(nchain):
        ck = jnp.concatenate([r[...] for r in ck_refs[c * per:(c + 1) * per]], axis=0).astype(BF16)
        kp = jnp.concatenate([r[...] for r in kp_refs[c * per:(c + 1) * per]], axis=1).astype(BF16)
        s = _dot_nt(ql_ref[...], ck) + _dot(qp_ref[...], kp)
        m_ref[c], l_ref[c], acc_ref[c] = update(s, ck, (m_ref[c], l_ref[c], acc_ref[c]))

    @pl.when(step == pl.num_programs(1) - 1)
    def _():
        m = m_ref[0]
        for c in range(1, nchain):
            m = jnp.maximum(m, m_ref[c])
        l = jnp.zeros((rows, 1), F32)
        acc = jnp.zeros((rows, KV_LORA), F32)
        for c in range(nchain):
            w = jnp.exp(m_ref[c] - m)
            l = l + w * l_ref[c]
            acc = acc + w * acc_ref[c]
        o_ref[...] = _uv_project(acc / l, wuv_ref, tn).astype(o_ref.dtype)


def _mla_sample_attn(page_table, qlat, qpe, ckn, kpn, wuv, cache_ckv, cache_kpe, tn, g):
    bs, n_pages = page_table.shape
    rows = H_A * tn
    pt = page_table.reshape(-1)
    nchain = 2 if g % 2 == 0 else 1

    def page_map(i):
        return lambda b, s, pt_ref: (pt_ref[b * n_pages + s * g + i], 0, 0)

    in_specs = [pl.BlockSpec((H_A, tn, KV_LORA), lambda b, s, pt_ref: (0, b, 0)),
                pl.BlockSpec((tn, ROPE_W), lambda b, s, pt_ref: (b, 0)),
                pl.BlockSpec((tn, KV_LORA), lambda b, s, pt_ref: (b, 0)),
                pl.BlockSpec((tn, ROPE_A), lambda b, s, pt_ref: (b, 0)),
                pl.BlockSpec(wuv.shape, lambda b, s, pt_ref: (0, 0, 0))]
    in_specs += [pl.BlockSpec((None, PAGE, KV_LORA), page_map(i)) for i in range(g)]
    in_specs += [pl.BlockSpec((None, ROPE_A, PAGE), page_map(i)) for i in range(g)]
    return pl.pallas_call(
        functools.partial(_mla_sample_body, tn=tn, g=g, nchain=nchain),
        grid_spec=pltpu.PrefetchScalarGridSpec(
            num_scalar_prefetch=1,
            grid=(bs, n_pages // g),
            in_specs=in_specs,
            out_specs=pl.BlockSpec((tn, H_A * V_A), lambda b, s, pt_ref: (b, 0)),
            scratch_shapes=[pltpu.VMEM((rows, KV_LORA), BF16),
                            pltpu.VMEM((rows, ROPE_A), BF16),
                            pltpu.VMEM((PAGE, KV_LORA), F32),
                            pltpu.VMEM((PAGE, ROPE_A), F32),
                            pltpu.VMEM((nchain, rows, 1), F32),
                            pltpu.VMEM((nchain, rows, 1), F32),
                            pltpu.VMEM((nchain, rows, KV_LORA), F32)]),
        out_shape=jax.ShapeDtypeStruct((bs * tn, H_A * V_A), F32),
        compiler_params=_cparams("parallel", "arbitrary"),
        name="mla_sample_attn",
    )(pt, qlat, qpe, ckn, kpn, wuv, *([cache_ckv] * g), *([cache_kpe] * g))


def _sb_block(qs, k, v, rest, acc, upper, mask=None):
    z = _dot_nt(qs, k)
    l_neg = _log_sigmoid(-z)
    if mask is not None:
        l_neg = jnp.where(mask, l_neg, 0.0)
    hi, lo = _split_bf16(l_neg, 2)
    later = _dot(hi, upper) + _dot(lo, upper) + rest
    a = jnp.exp(z + l_neg + later)
    if mask is not None:
        a = jnp.where(mask, a, 0.0)
    acc = acc + _dot(a.astype(BF16), v)
    rest = rest + jnp.sum(l_neg, axis=-1, keepdims=True)
    return rest, acc


def _upper_ones(n):
    r = lax.broadcasted_iota(jnp.int32, (n, n), 0)
    c = lax.broadcasted_iota(jnp.int32, (n, n), 1)
    return jnp.where(r > c, 1.0, 0.0).astype(BF16)


def _head_collapse(acc, t, width, n_heads):
    lane = lax.broadcasted_iota(jnp.int32, (t, n_heads * width), 1)
    out = jnp.zeros((t, n_heads * width), F32)
    for h in range(n_heads):
        out = out + jnp.where((lane >= h * width) & (lane < (h + 1) * width), acc[h * t:(h + 1) * t], 0.0)
    return out


def _sb_prompt_body(q_ref, k_ref, v_ref, o_ref, kb_ref, vb_ref, rest_ref, acc_ref, *, tq, tk):
    qi = pl.program_id(1)
    rows = H_D * tq

    @pl.when(qi == 0)
    def _():
        kb_ref[...] = k_ref[...].astype(BF16)
        vb_ref[...] = v_ref[...].astype(BF16)

    qs = _head_stack(q_ref[...] * SB_SCALE, DH_D, H_D).astype(BF16)
    upper = _upper_ones(tk)
    rest_ref[...] = jnp.zeros(rest_ref.shape, F32)
    acc_ref[...] = jnp.zeros(acc_ref.shape, F32)

    def block(k0, masked):
        mask = None
        if masked:
            row = lax.broadcasted_iota(jnp.int32, (rows, tk), 0) & (tq - 1)
            col = lax.broadcasted_iota(jnp.int32, (rows, tk), 1)
            mask = col + k0 < row + qi * tq
        rest, acc = _sb_block(qs, kb_ref[pl.ds(k0, tk), :], vb_ref[pl.ds(k0, tk), :],
                              rest_ref[...], acc_ref[...], upper, mask)
        rest_ref[...] = rest
        acc_ref[...] = acc

    nfull = (qi * tq) // tk
    block(pl.multiple_of(nfull * tk, tk), True)

    def full_block(i, carry):
        block(pl.multiple_of((nfull - 1 - i) * tk, tk), False)
        return carry

    lax.fori_loop(0, nfull, full_block, 0)
    o_ref[...] = _head_collapse(acc_ref[...], tq, DH_D, H_D).astype(o_ref.dtype)


def _sb_prompt_attn(q, k, v, b, t):
    tq = min(128, t)
    tk = min(256, t)
    nq = t // tq
    rows = H_D * tq
    w = H_D * DH_D
    return pl.pallas_call(
        functools.partial(_sb_prompt_body, tq=tq, tk=tk),
        grid=(b, nq),
        in_specs=[pl.BlockSpec((tq, w), lambda bi, qi: (bi * nq + qi, 0)),
                  pl.BlockSpec((t, w), lambda bi, qi: (bi, 0)),
                  pl.BlockSpec((t, w), lambda bi, qi: (bi, 0))],
        out_specs=pl.BlockSpec((tq, w), lambda bi, qi: (bi * nq + qi, 0)),
        out_shape=jax.ShapeDtypeStruct((b * t, w), BF16),
        scratch_shapes=[pltpu.VMEM((t, w), BF16),
                        pltpu.VMEM((t, w), BF16),
                        pltpu.VMEM((rows, 1), F32),
                        pltpu.VMEM((rows, w), F32)],
        compiler_params=_cparams("parallel", "arbitrary"),
        name="sb_prompt_attn",
    )(q, k, v)


def _sb_sample_body(pt_ref, q_ref, kn_ref, vn_ref, *rest, tn, g):
    k_refs, v_refs = rest[:g], rest[g:2 * g]
    o_ref, qs_ref, kpad_ref, vpad_ref, rest_ref, acc_ref = rest[2 * g:]
    step = pl.program_id(1)
    rows = H_D * tn
    r = lax.broadcasted_iota(jnp.int32, (PAGE, 2 * PAGE), 0)
    c = lax.broadcasted_iota(jnp.int32, (PAGE, 2 * PAGE), 1)
    sums = jnp.where((c >= PAGE) | (r > c), 1.0, 0.0).astype(BF16)

    def page_sums(l_negs):
        split = [_split_bf16(x, 2) for x in l_negs]
        stacked = jnp.concatenate([s[0] for s in split] + [s[1] for s in split], axis=0)
        cs = _dot(stacked, sums)
        n = len(l_negs) * rows
        cs = cs[:n] + cs[n:]
        return [(cs[i * rows:(i + 1) * rows, :PAGE], cs[i * rows:(i + 1) * rows, PAGE:])
                for i in range(len(l_negs))]

    @pl.when(step == 0)
    def _():
        qs_ref[...] = _head_stack(q_ref[...] * SB_SCALE, DH_D, H_D).astype(BF16)
        kpad_ref[...] = jnp.zeros(kpad_ref.shape, F32)
        vpad_ref[...] = jnp.zeros(vpad_ref.shape, F32)
        kpad_ref[0:tn, :] = kn_ref[...]
        vpad_ref[0:tn, :] = vn_ref[...]
        row = lax.broadcasted_iota(jnp.int32, (rows, PAGE), 0) & (tn - 1)
        col = lax.broadcasted_iota(jnp.int32, (rows, PAGE), 1)
        mask = col < row
        z = _dot_nt(qs_ref[...], kpad_ref[...].astype(BF16))
        l_neg = jnp.where(mask, _log_sigmoid(-z), 0.0)
        (later, total), = page_sums([l_neg])
        a = jnp.where(mask, jnp.exp(z + l_neg + later), 0.0)
        acc_ref[...] = _dot(a.astype(BF16), vpad_ref[...].astype(BF16))
        rest_ref[...] = total

    kt = jnp.concatenate([x[...] for x in k_refs], axis=1).astype(BF16)
    vt = jnp.concatenate([x[...] for x in v_refs], axis=1).astype(BF16)
    z = _dot(qs_ref[...], kt)
    l_neg = _log_sigmoid(-z)
    zl = z + l_neg
    stick = rest_ref[...]
    logits = []
    for i, (later, total) in enumerate(page_sums([l_neg[:, i * PAGE:(i + 1) * PAGE] for i in range(g)])):
        logits.append(zl[:, i * PAGE:(i + 1) * PAGE] + later + stick)
        stick = stick + total
    a = jnp.exp(jnp.concatenate(logits, axis=1)).astype(BF16)
    acc_ref[...] += _dot_nt(a, vt)
    rest_ref[...] = stick

    @pl.when(step == pl.num_programs(1) - 1)
    def _():
        o_ref[...] = _head_collapse(acc_ref[...], tn, DH_D, H_D).astype(o_ref.dtype)


def _sb_sample_attn(page_table, q, kn, vn, cache_k, cache_v, tn, g):
    bs, n_pages = page_table.shape
    rows = H_D * tn
    w = H_D * DH_D
    pt = page_table.reshape(-1)

    def page_map(i):
        return lambda b, s, pt_ref: (pt_ref[b * n_pages + n_pages - 1 - (s * g + i)], 0, 0)

    in_specs = [pl.BlockSpec((tn, w), lambda b, s, pt_ref: (b, 0))] * 3
    in_specs += [pl.BlockSpec((None, w, PAGE), page_map(i)) for i in range(g)] * 2
    return pl.pallas_call(
        functools.partial(_sb_sample_body, tn=tn, g=g),
        grid_spec=pltpu.PrefetchScalarGridSpec(
            num_scalar_prefetch=1,
            grid=(bs, n_pages // g),
            in_specs=in_specs,
            out_specs=pl.BlockSpec((tn, w), lambda b, s, pt_ref: (b, 0)),
            scratch_shapes=[pltpu.VMEM((rows, w), BF16),
                            pltpu.VMEM((PAGE, w), F32),
                            pltpu.VMEM((PAGE, w), F32),
                            pltpu.VMEM((rows, PAGE), F32),
                            pltpu.VMEM((rows, w), F32)]),
        out_shape=jax.ShapeDtypeStruct((bs * tn, w), F32),
        compiler_params=_cparams("parallel", "arbitrary"),
        name="sb_sample_attn",
    )(pt, q, kn, vn, *([cache_k] * g), *([cache_v] * g))


def _gated_body(*refs, mode, lq, nblk, ngrp, dk, dv, dkp, dvp, has_s0):
    nh = 4
    nsq = CHUNK // lq
    it = iter(refs)
    if mode == "hgrn":
        zq_ref, zf_ref, zv_ref, zg_ref, lb_ref, nw_ref = (next(it) for _ in range(6))
    else:
        zq_ref, zk_ref, zv_ref, zg_ref, ga_ref, wg_hi_ref, wg_lo_ref, bg_ref, nw_ref = (next(it) for _ in range(9))
    s0_ref = next(it) if has_s0 else None
    o_ref, sout_ref, s_ref = next(it), next(it), next(it)
    step = pl.program_id(1)

    if has_s0:
        s_ref[...] = jnp.zeros(s_ref.shape, F32)
        s_ref[:, :, 0:dk, 0:dv] = s0_ref[...]
    else:
        @pl.when(step == 0)
        def _():
            s_ref[...] = jnp.zeros(s_ref.shape, F32)

    r = lax.broadcasted_iota(jnp.int32, (CHUNK, CHUNK), 0)
    c = lax.broadcasted_iota(jnp.int32, (CHUNK, CHUNK), 1)
    seq_r, seq_c = r // lq, c // lq
    same = seq_r == seq_c
    causal = same & (r >= c)
    ref_pos = seq_r * lq + lq // 2
    ones = lambda m: jnp.where(m, 1.0, 0.0).astype(BF16)
    sums = jnp.concatenate([ones(causal), ones(same & (c <= ref_pos)), ones(same)], axis=0)
    row_seq = lax.broadcasted_iota(jnp.int32, (CHUNK, dkp), 0) // lq

    def chunk(gi, ci):
        rows = pl.ds(pl.multiple_of(ci * CHUNK, CHUNK), CHUNK)
        if mode == "hgrn":
            lb = lb_ref[...]
            f = lb + (1.0 - lb) * _sigmoid(zf_ref[gi, rows, :])
            logf, k = jnp.log(f), 1.0 - f
            zq = zq_ref[gi, rows, :]
            q = zq * _sigmoid(zq)
        else:
            ga_hi, ga_lo = _split_bf16(ga_ref[gi, rows, :], 2)
            logits = (_dot(ga_hi, wg_hi_ref[...]) + _dot(ga_hi, wg_lo_ref[...])
                      + _dot(ga_lo, wg_hi_ref[...]) + bg_ref[...])
            logf = _log_sigmoid(logits) * (1.0 / GATE_NORM)
            k = zk_ref[gi, rows, :]
            q = zq_ref[gi, rows, :] * (dk ** -0.5)
        v = zv_ref[gi, rows, :].astype(BF16)
        parts = _split_bf16(logf, 3)
        gall = _dot(sums, parts[0]) + _dot(sums, parts[1]) + _dot(sums, parts[2])
        g, gr, gl = gall[:CHUNK], gall[CHUNK:2 * CHUNK], gall[2 * CHUNK:]
        qg = (q * jnp.exp(g - gr)).astype(BF16)
        kg = (k * jnp.exp(gr - g)).astype(BF16)
        qs = q * jnp.exp(g)
        kd = k * jnp.exp(gl - g)
        el = jnp.exp(gl)
        gate = zg_ref[gi, rows, :]
        outs = []
        for h in range(nh):
            ks, vs = slice(h * dkp, (h + 1) * dkp), slice(h * dvp, (h + 1) * dvp)
            a = jnp.where(causal, _dot_nt(qg[:, ks], kg[:, ks]), 0.0)
            o = _dot(a.astype(BF16), v[:, vs])
            for s in range(nsq):
                if nsq == 1:
                    qs_s, kd_s = qs[:, ks], kd[:, ks]
                else:
                    qs_s = jnp.where(row_seq == s, qs[:, ks], 0.0)
                    kd_s = jnp.where(row_seq == s, kd[:, ks], 0.0)
                st = s_ref[gi * nsq + s, h]
                o = o + _dot(qs_s.astype(BF16), st.astype(BF16))
                e_col = jnp.broadcast_to(el[s * lq:s * lq + 1, ks], (dkp, dkp)).T
                if dvp > dkp:
                    e_col = jnp.concatenate([e_col] * (dvp // dkp), axis=1)
                s_ref[gi * nsq + s, h] = e_col * st + _dot_tn(kd_s.astype(BF16), v[:, vs])
            nwh = nw_ref[...]
            gh = gate[:, vs]
            outs.append(_rms(o, nwh, dv) * (gh * _sigmoid(gh)))
        o_ref[gi, rows, :] = jnp.concatenate(outs, axis=-1).astype(o_ref.dtype)

    if nblk == 1:
        for gi in range(ngrp):
            chunk(gi, 0)
    else:
        def loop_body(ci, carry):
            for gi in range(ngrp):
                chunk(gi, ci)
            return carry
        lax.fori_loop(0, nblk, loop_body, 0)

    if has_s0:
        sout_ref[...] = s_ref[:, :, 0:dk, 0:dv]
    else:
        @pl.when(step == pl.num_programs(1) - 1)
        def _():
            sout_ref[...] = s_ref[:, :, 0:dk, 0:dv]


def _gated_linear(mode, arrays, col_blocks, params, s0, b, t, dk, dv, dkp, dvp):
    nh = 4
    m = b * t
    lq = min(CHUNK, t)
    nsq = CHUNK // lq
    if s0 is None:
        n_groups, group_rows = b, t
        nblk = min(8, t // CHUNK)
    else:
        n_groups, group_rows = m // CHUNK, CHUNK
        nblk = 1
    ngrp = 2 if n_groups % 2 == 0 else 1
    tr = CHUNK * nblk
    grid = (n_groups // ngrp, group_rows // tr)
    state_map = lambda bi, si: (bi, 0, 0, 0)
    arrays = [a.reshape(n_groups, group_rows, a.shape[-1]) for a in arrays]
    in_specs = [pl.BlockSpec((ngrp, tr, w), functools.partial(lambda bi, si, cb: (bi, si, cb), cb=cb))
                for (w, cb) in col_blocks]
    in_specs += [pl.BlockSpec(p.shape, lambda bi, si: (0, 0)) for p in params]
    operands = list(arrays) + list(params)
    if s0 is not None:
        in_specs.append(pl.BlockSpec((ngrp * nsq, nh, dk, dv), state_map))
        operands.append(s0)
    o, s_out = pl.pallas_call(
        functools.partial(_gated_body, mode=mode, lq=lq, nblk=nblk, ngrp=ngrp, dk=dk, dv=dv, dkp=dkp,
                          dvp=dvp, has_s0=s0 is not None),
        grid=grid,
        in_specs=in_specs,
        out_specs=[pl.BlockSpec((ngrp, tr, nh * dvp), lambda bi, si: (bi, si, 0)),
                   pl.BlockSpec((ngrp * nsq, nh, dk, dv), state_map)],
        out_shape=[jax.ShapeDtypeStruct((n_groups, group_rows, nh * dvp), BF16),
                   jax.ShapeDtypeStruct((b, nh, dk, dv), F32)],
        scratch_shapes=[pltpu.VMEM((ngrp * nsq, nh, dkp, dvp), F32)],
        compiler_params=_cparams("parallel", "arbitrary"),
        name="gated_linear_" + mode,
    )(*operands)
    return o.reshape(m, nh * dvp), s_out


def _ffn_body(x_ref, oa_ref, ob_ref, woa_ref, wob_ref, fnw_ref, wg_ref, wu_ref, wo_ref, finw_ref,
              y_ref, h_ref, xn_ref, acc_ref, *, final_norm):
    j = pl.program_id(1)

    @pl.when(j == 0)
    def _():
        h = (x_ref[...] + _dot(oa_ref[...].astype(BF16), woa_ref[...])
             + _dot(ob_ref[...].astype(BF16), wob_ref[...]))
        h_ref[...] = h
        xn_ref[...] = _rms(h, fnw_ref[...]).astype(BF16)
        acc_ref[...] = jnp.zeros(acc_ref.shape, F32)

    xn = xn_ref[...]
    gate = _dot(xn, wg_ref[...])
    up = _dot(xn, wu_ref[...])
    act = (gate * _sigmoid(gate) * up).astype(BF16)
    acc_ref[...] += _dot(act, wo_ref[...])

    @pl.when(j == pl.num_programs(1) - 1)
    def _():
        y = h_ref[...] + acc_ref[...]
        if final_norm:
            y = _rms(y, finw_ref[...])
        y_ref[...] = y


def _mix_ffn(x, oa, ob, woa, wob, fnw, w_in, w_out, finw, final_norm):
    m = x.shape[0]
    tm = min(512, m)
    tf = D_FF // 2
    nf = D_FF // tf
    ka, kb = oa.shape[1], ob.shape[1]
    return pl.pallas_call(
        functools.partial(_ffn_body, final_norm=final_norm),
        grid=(m // tm, nf),
        in_specs=[pl.BlockSpec((tm, D_MODEL), lambda i, j: (i, 0)),
                  pl.BlockSpec((tm, ka), lambda i, j: (i, 0)),
                  pl.BlockSpec((tm, kb), lambda i, j: (i, 0)),
                  pl.BlockSpec((ka, D_MODEL), lambda i, j: (0, 0)),
                  pl.BlockSpec((kb, D_MODEL), lambda i, j: (0, 0)),
                  pl.BlockSpec((1, D_MODEL), lambda i, j: (0, 0)),
                  pl.BlockSpec((D_MODEL, tf), lambda i, j: (0, j)),
                  pl.BlockSpec((D_MODEL, tf), lambda i, j: (0, j + nf)),
                  pl.BlockSpec((tf, D_MODEL), lambda i, j: (j, 0)),
                  pl.BlockSpec((1, D_MODEL), lambda i, j: (0, 0))],
        out_specs=pl.BlockSpec((tm, D_MODEL), lambda i, j: (i, 0)),
        out_shape=jax.ShapeDtypeStruct((m, D_MODEL), F32),
        scratch_shapes=[pltpu.VMEM((tm, D_MODEL), F32),
                        pltpu.VMEM((tm, D_MODEL), BF16),
                        pltpu.VMEM((tm, D_MODEL), F32)],
        compiler_params=_cparams("parallel", "arbitrary"),
        name="mix_ffn",
    )(x, oa, ob, woa, wob, fnw.reshape(1, -1), w_in, w_in, w_out, finw.reshape(1, -1))


def _pad_heads(w, nh, d, dp, axis):
    shape = w.shape
    w = w.reshape(shape[:axis] + (nh, d) + shape[axis + 1:])
    pad = [(0, 0)] * w.ndim
    pad[axis + 1] = (0, dp - d)
    w = jnp.pad(w, pad)
    return w.reshape(shape[:axis] + (nh * dp,) + shape[axis + 1:])


def _rope_tables(pos):
    inv = ROPE_THETA ** (-jnp.arange(0, ROPE_A, 2, dtype=F32) / ROPE_A)
    ang = pos[:, None] * inv[None, :]
    cos = jnp.repeat(jnp.cos(ang), 2, axis=1)
    sin = jnp.repeat(jnp.sin(ang), 2, axis=1) * jnp.tile(jnp.array([-1.0, 1.0], F32), ROPE_A // 2)[None, :]
    return jnp.tile(cos, (1, H_A)), jnp.tile(sin, (1, H_A))


def _prepare_weights(P):
    W = {}
    w_ab = P["w_in_ab"][0]
    o_ckv, o_kpe, o_h = Q_LORA, Q_LORA + KV_LORA, Q_LORA + KV_LORA + ROPE_A
    W["w_mla"] = jnp.concatenate(
        [w_ab[:, :o_kpe], jnp.tile(w_ab[:, o_kpe:o_h], (1, H_A))], axis=1).astype(BF16)
    W["w_hgrn"] = w_ab[:, o_h:].astype(BF16)
    wuq = P["mla_w_uq"][0].reshape(Q_LORA, H_A, NOPE_A + ROPE_A)
    W["w_uq"] = jnp.concatenate([wuq[:, :, :NOPE_A].reshape(Q_LORA, -1),
                                 wuq[:, :, NOPE_A:].reshape(Q_LORA, -1)], axis=1).astype(BF16)
    half = (jnp.arange(H_A) % 2)[:, None, None]
    wk = jnp.transpose(P["mla_w_uk"][0], (1, 2, 0))
    W["w_uk"] = jnp.where(half == 0, jnp.pad(wk, ((0, 0), (0, NOPE_A), (0, 0))),
                          jnp.pad(wk, ((0, 0), (NOPE_A, 0), (0, 0)))).astype(BF16)
    W["w_uk_t"] = jnp.transpose(W["w_uk"], (0, 2, 1))
    wv = jnp.transpose(P["mla_w_uv"][0], (1, 0, 2))
    W["w_uv"] = jnp.where(half == 0, jnp.pad(wv, ((0, 0), (0, 0), (0, V_A))),
                          jnp.pad(wv, ((0, 0), (0, 0), (V_A, 0)))).astype(BF16)
    lb = jnp.cumsum(jax.nn.softmax(P["hgrn_lb_logits"].astype(F32), axis=0), axis=0)[0]
    W["hgrn_lb"] = lb.reshape(1, -1)
    W["hgrn_nw"] = P["hgrn_norm_w"][0].reshape(1, -1)
    w_out_ab = P["w_out_ab"][0].astype(BF16)
    W["w_out_a"], W["w_out_b"] = w_out_ab[:H_A * V_A], w_out_ab[H_A * V_A:]

    w_cd = P["w_in_cd"][0]
    offs = [0]
    for wdt in (H_C * DK_C, H_C * DK_C, H_C * DV_C, H_C * DV_C, GATE_RANK, H_D * DH_D * 3):
        offs.append(offs[-1] + wdt)
    seg = lambda i: w_cd[:, offs[i]:offs[i + 1]]
    W["w_gla"] = jnp.concatenate(
        [_pad_heads(seg(0), H_C, DK_C, DKP_C, 1), _pad_heads(seg(1), H_C, DK_C, DKP_C, 1),
         _pad_heads(seg(2), H_C, DV_C, DVP_C, 1), _pad_heads(seg(3), H_C, DV_C, DVP_C, 1),
         jnp.pad(seg(4), ((0, 0), (0, GATE_RANK_P - GATE_RANK)))], axis=1).astype(BF16)
    W["w_sb"] = seg(5).astype(BF16)
    wg2 = jnp.pad(_pad_heads(P["gla_w_gate2"][0], H_C, DK_C, DKP_C, 1), ((0, GATE_RANK_P - GATE_RANK), (0, 0)))
    W["wg2_hi"] = wg2.astype(BF16)
    W["wg2_lo"] = (wg2 - W["wg2_hi"].astype(F32)).astype(BF16)
    W["bg2"] = _pad_heads(P["gla_b_gate2"][0].reshape(1, -1), H_C, DK_C, DKP_C, 1)
    W["gla_nw"] = jnp.pad(P["gla_norm_w"][0], (0, DVP_C - DV_C)).reshape(1, -1)
    w_out_cd = P["w_out_cd"][0]
    W["w_out_c"] = _pad_heads(w_out_cd[:H_C * DV_C], H_C, DV_C, DVP_C, 0).astype(BF16)
    W["w_out_d"] = w_out_cd[H_C * DV_C:].astype(BF16)
    W["w_ffn_in"] = [P["w_ffn_in"][l].astype(BF16) for l in range(2)]
    W["w_ffn_out"] = [P["w_ffn_out"][l].astype(BF16) for l in range(2)]
    return W


def _trunk(x3, pos, P, W, ctx):
    b, t, _ = x3.shape
    m = b * t
    x = x3.reshape(m, D_MODEL)
    sample = ctx is not None
    cos, sin = _rope_tables(pos)
    reps = min(512, m) // t
    if reps > 1:
        cos, sin = jnp.tile(cos, (reps, 1)), jnp.tile(sin, (reps, 1))

    qlat, qpe, ckv, kpe, kpet = _mla_prep(
        x, P["attn_norm_w"][0], W["w_mla"], P["mla_q_norm_w"][0], W["w_uq"],
        W["w_uk"] if sample else W["w_uk_t"], P["mla_kv_norm_w"][0], cos, sin,
        F32 if sample else BF16, transposed_q=not sample)
    zh = _norm_matmul(x, P["attn_norm_w"][0], W["w_hgrn"], H_B * DK_B, split=True)
    if sample:
        o_a = _mla_sample_attn(ctx["page_table"], qlat, qpe, ckv, kpe, W["w_uv"],
                               ctx["cache_mla_ckv"], ctx["cache_mla_kpe"], t, ctx["pages_per_step"])
        s0 = ctx["state_hgrn"][0]
    else:
        o_a = _mla_prompt_attn(qlat, qpe, ckv, kpet, W["w_uv"], b, t)
        s0 = None
    wb = H_B * DK_B
    o_b, s_hgrn = _gated_linear("hgrn", zh, [(wb, 0)] * 4, [W["hgrn_lb"], W["hgrn_nw"]], s0,
                                b, t, DK_B, DV_B, DK_B, DV_B)
    h1 = _mix_ffn(x, o_a, o_b, W["w_out_a"], W["w_out_b"], P["ffn_norm_w"][0],
                  W["w_ffn_in"][0], W["w_ffn_out"][0], P["final_norm_w"], final_norm=False)

    zg = _norm_matmul(h1, P["attn_norm_w"][1], W["w_gla"], 640, split=False)
    sq, sk, sv = _norm_matmul(h1, P["attn_norm_w"][1], W["w_sb"], H_D * DH_D, split=True)
    if sample:
        o_d = _sb_sample_attn(ctx["page_table"], sq, sk, sv, ctx["cache_sb_k"], ctx["cache_sb_v"],
                              t, ctx["pages_per_step"])
        s0 = ctx["state_gla"][0]
    else:
        o_d = _sb_prompt_attn(sq, sk, sv, b, t)
        s0 = None
    wq, wv = H_C * DKP_C, H_C * DVP_C
    o_c, s_gla = _gated_linear(
        "gla", [zg] * 5, [(wq, 0), (wq, 1), (wv, 1), (wv, 2), (GATE_RANK_P, (2 * wq + 2 * wv) // GATE_RANK_P)],
        [W["wg2_hi"], W["wg2_lo"], W["bg2"], W["gla_nw"]], s0, b, t, DK_C, DV_C, DKP_C, DVP_C)
    y = _mix_ffn(h1, o_c, o_d, W["w_out_c"], W["w_out_d"], P["ffn_norm_w"][1],
                 W["w_ffn_in"][1], W["w_ffn_out"][1], P["final_norm_w"], final_norm=True)

    new = (ckv.reshape(1, b, t, KV_LORA), kpe.reshape(1, b, t, ROPE_A), s_hgrn[None],
           sk.reshape(1, b, t, H_D, DH_D), sv.reshape(1, b, t, H_D, DH_D), s_gla[None])
    return y.reshape(b, t, D_MODEL), new


def kernel(x_prompt, x_sample, cache_mla_ckv, cache_mla_kpe, state_hgrn, cache_sb_k, cache_sb_v, state_gla, page_table, attn_norm_w, ffn_norm_w, final_norm_w, w_in_ab, mla_q_norm_w, mla_w_uq, mla_kv_norm_w, mla_w_uk, mla_w_uv, hgrn_lb_logits, hgrn_norm_w, w_out_ab, w_in_cd, gla_w_gate2, gla_b_gate2, gla_norm_w, w_out_cd, w_ffn_in, w_ffn_out):
    P = {
        "attn_norm_w": attn_norm_w, "ffn_norm_w": ffn_norm_w, "final_norm_w": final_norm_w,
        "w_in_ab": w_in_ab, "mla_q_norm_w": mla_q_norm_w, "mla_w_uq": mla_w_uq,
        "mla_kv_norm_w": mla_kv_norm_w, "mla_w_uk": mla_w_uk, "mla_w_uv": mla_w_uv,
        "hgrn_lb_logits": hgrn_lb_logits, "hgrn_norm_w": hgrn_norm_w, "w_out_ab": w_out_ab,
        "w_in_cd": w_in_cd, "gla_w_gate2": gla_w_gate2, "gla_b_gate2": gla_b_gate2,
        "gla_norm_w": gla_norm_w, "w_out_cd": w_out_cd, "w_ffn_in": w_ffn_in, "w_ffn_out": w_ffn_out,
    }
    W = _prepare_weights(P)
    n_pages = page_table.shape[1]
    past = n_pages * PAGE
    pos_prompt = jnp.arange(x_prompt.shape[1], dtype=F32)
    pos_sample = jnp.arange(x_sample.shape[1], dtype=F32) + float(past)
    n_pool = cache_sb_k.shape[1]
    ctx = {
        "page_table": page_table,
        "cache_mla_ckv": cache_mla_ckv[0],
        "cache_mla_kpe": jnp.transpose(cache_mla_kpe[0], (0, 2, 1)),
        "state_hgrn": state_hgrn,
        "cache_sb_k": jnp.transpose(cache_sb_k[0], (0, 2, 3, 1)).reshape(n_pool, H_D * DH_D, PAGE),
        "cache_sb_v": jnp.transpose(cache_sb_v[0], (0, 2, 3, 1)).reshape(n_pool, H_D * DH_D, PAGE),
        "state_gla": state_gla,
        "pages_per_step": min(32, n_pages),
    }
    y_p, new_p = _trunk(x_prompt, pos_prompt, P, W, None)
    y_s, new_s = _trunk(x_sample, pos_sample, P, W, ctx)
    return (y_p, y_s) + new_p + new_s
```

```python
import functools

import jax
import jax.numpy as jnp
from jax import lax
from jax.experimental import pallas as pl
from jax.experimental.pallas import tpu as pltpu

F32 = jnp.float32
BF16 = jnp.bfloat16

D_MODEL = 1024
PAGE = 128
H_A, NOPE_A, ROPE_A, V_A = 8, 64, 32, 64
Q_LORA, KV_LORA = 384, 256
ROPE_THETA = 10000.0
MLA_SCALE = (NOPE_A + ROPE_A) ** -0.5
H_B, DK_B, DV_B = 4, 128, 128
H_C, DK_C, DV_C = 4, 96, 192
DKP_C, DVP_C = 128, 256
GATE_RANK, GATE_RANK_P = 16, 128
GATE_NORM = 16.0
H_D, DH_D = 4, 64
SB_SCALE = DH_D ** -0.5
D_FF = 2816
EPS = 1e-6
CHUNK = 64
ROPE_W = H_A * ROPE_A

VMEM_LIMIT = 56 * 1024 * 1024


def _cparams(*sem):
    return pltpu.CompilerParams(dimension_semantics=sem, vmem_limit_bytes=VMEM_LIMIT)


def _dot(a, b):
    return jnp.dot(a, b, preferred_element_type=F32)


def _dot_nt(a, b):
    return lax.dot_general(a, b, (((1,), (1,)), ((), ())), preferred_element_type=F32)


def _dot_tn(a, b):
    return lax.dot_general(a, b, (((0,), (0,)), ((), ())), preferred_element_type=F32)


def _split_bf16(x, n):
    parts = []
    for _ in range(n - 1):
        p = x.astype(BF16)
        parts.append(p)
        x = x - p.astype(F32)
    parts.append(x.astype(BF16))
    return parts


def _rms(x, w, n=None):
    n = x.shape[-1] if n is None else n
    ms = jnp.sum(x * x, axis=-1, keepdims=True) * (1.0 / n)
    return x * lax.rsqrt(ms + EPS) * w


def _sigmoid(x):
    return 1.0 / (1.0 + jnp.exp(-x))


def _log_sigmoid(x):
    return jnp.minimum(x, 0.0) - jnp.log(1.0 + jnp.exp(-jnp.abs(x)))


def _rope(x, cos, sin_signed):
    n = x.shape[-1]
    lane = lax.broadcasted_iota(jnp.int32, x.shape, x.ndim - 1)
    nxt = pltpu.roll(x, n - 1, x.ndim - 1)
    prv = pltpu.roll(x, 1, x.ndim - 1)
    swap = jnp.where((lane & 1) == 0, nxt, prv)
    return x * cos + swap * sin_signed


def _norm_matmul_body(x_ref, nw_ref, w_ref, *rest, n_out):
    outs, xn_ref = rest[:n_out], rest[n_out]
    j = pl.program_id(1)

    @pl.when(j == 0)
    def _():
        xn_ref[...] = _rms(x_ref[...], nw_ref[...]).astype(BF16)

    z = _dot(xn_ref[...], w_ref[...])
    if n_out == 1:
        outs[0][...] = z
    else:
        for k in range(n_out):
            @pl.when(j == k)
            def _(k=k):
                outs[k][...] = z


def _norm_matmul(x, nw, w, tn, split):
    m, kdim = x.shape
    n = w.shape[1]
    tm = min(1024, m)
    nj = n // tn
    n_out = nj if split else 1
    if split:
        out_shape = [jax.ShapeDtypeStruct((m, tn), F32)] * nj
        out_specs = [pl.BlockSpec((tm, tn), lambda i, j: (i, 0))] * nj
    else:
        out_shape = [jax.ShapeDtypeStruct((m, n), F32)]
        out_specs = [pl.BlockSpec((tm, tn), lambda i, j: (i, j))]
    outs = pl.pallas_call(
        functools.partial(_norm_matmul_body, n_out=n_out),
        grid=(m // tm, nj),
        in_specs=[pl.BlockSpec((tm, kdim), lambda i, j: (i, 0)),
                  pl.BlockSpec((1, kdim), lambda i, j: (0, 0)),
                  pl.BlockSpec((kdim, tn), lambda i, j: (0, j))],
        out_specs=out_specs,
        out_shape=out_shape,
        scratch_shapes=[pltpu.VMEM((tm, kdim), BF16)],
        compiler_params=_cparams("parallel", "arbitrary"),
        name="norm_matmul",
    )(x, nw.reshape(1, kdim), w)
    return outs if split else outs[0]


def _mla_prep_body(x_ref, anw_ref, w_ref, qnw_ref, wuq_ref, wk_ref, kvnw_ref, cos_ref, sin_ref,
                   qlat_ref, qpe_ref, ckv_ref, kpe_ref, kpet_ref, *, transposed_q):
    xn = _rms(x_ref[...], anw_ref[...]).astype(BF16)
    z = _dot(xn, w_ref[...])
    cqn = _rms(z[:, :Q_LORA], qnw_ref[...]).astype(BF16)
    q = _dot(cqn, wuq_ref[...])
    qn = q[:, :H_A * NOPE_A].astype(BF16)
    for h in range(H_A):
        qpair = qn[:, 128 * (h // 2):128 * (h // 2 + 1)]
        ql = _dot_nt(wk_ref[h], qpair) if transposed_q else _dot(qpair, wk_ref[h])
        qlat_ref[h] = (ql * MLA_SCALE).astype(qlat_ref.dtype)
    cos, sin = cos_ref[...], sin_ref[...]
    qr = _rope(q[:, H_A * NOPE_A:], cos, sin) * MLA_SCALE
    qpe_ref[...] = (qr.T if transposed_q else qr).astype(qpe_ref.dtype)
    ckv_ref[...] = _rms(z[:, Q_LORA:Q_LORA + KV_LORA], kvnw_ref[...])
    kr = _rope(z[:, Q_LORA + KV_LORA:], cos, sin)
    kpet_ref[...] = kr.astype(BF16)
    kpe_ref[...] = kr[:, :ROPE_A]


def _mla_prep(x, anw, w_mla, qnw, wuq, wk, kvnw, cos, sin, q_dtype, transposed_q):
    m = x.shape[0]
    tm = min(512, m)
    nt = cos.shape[0] // tm
    wz = w_mla.shape[1]
    if transposed_q:
        q_specs = [pl.BlockSpec((H_A, KV_LORA, tm), lambda i: (0, 0, i)),
                   pl.BlockSpec((ROPE_W, tm), lambda i: (0, i))]
        q_shapes = [jax.ShapeDtypeStruct((H_A, KV_LORA, m), q_dtype),
                    jax.ShapeDtypeStruct((ROPE_W, m), q_dtype)]
    else:
        q_specs = [pl.BlockSpec((H_A, tm, KV_LORA), lambda i: (0, i, 0)),
                   pl.BlockSpec((tm, ROPE_W), lambda i: (i, 0))]
        q_shapes = [jax.ShapeDtypeStruct((H_A, m, KV_LORA), q_dtype),
                    jax.ShapeDtypeStruct((m, ROPE_W), q_dtype)]
    return pl.pallas_call(
        functools.partial(_mla_prep_body, transposed_q=transposed_q),
        grid=(m // tm,),
        in_specs=[pl.BlockSpec((tm, D_MODEL), lambda i: (i, 0)),
                  pl.BlockSpec((1, D_MODEL), lambda i: (0, 0)),
                  pl.BlockSpec((D_MODEL, wz), lambda i: (0, 0)),
                  pl.BlockSpec((1, Q_LORA), lambda i: (0, 0)),
                  pl.BlockSpec(wuq.shape, lambda i: (0, 0)),
                  pl.BlockSpec(wk.shape, lambda i: (0, 0, 0)),
                  pl.BlockSpec((1, KV_LORA), lambda i: (0, 0)),
                  pl.BlockSpec((tm, ROPE_W), lambda i: (i % nt, 0)),
                  pl.BlockSpec((tm, ROPE_W), lambda i: (i % nt, 0))],
        out_specs=q_specs + [
                   pl.BlockSpec((tm, KV_LORA), lambda i: (i, 0)),
                   pl.BlockSpec((tm, ROPE_A), lambda i: (i, 0)),
                   pl.BlockSpec((tm, ROPE_W), lambda i: (i, 0))],
        out_shape=q_shapes + [
                   jax.ShapeDtypeStruct((m, KV_LORA), F32),
                   jax.ShapeDtypeStruct((m, ROPE_A), F32),
                   jax.ShapeDtypeStruct((m, ROPE_W), BF16)],
        compiler_params=_cparams("parallel"),
        name="mla_prep",
    )(x, anw.reshape(1, -1), w_mla, qnw.reshape(1, -1), wuq, wk, kvnw.reshape(1, -1), cos, sin)


def _head_stack(x, width, n_heads):
    lane = lax.broadcasted_iota(jnp.int32, x.shape, 1)
    zero = jnp.zeros_like(x)
    return jnp.concatenate(
        [jnp.where((lane >= h * width) & (lane < (h + 1) * width), x, zero) for h in range(n_heads)], axis=0)


def _uv_project(o, wuv_ref, t):
    ob = o.astype(BF16)
    pieces = []
    for p in range(H_A // 2):
        pieces.append(_dot(ob[2 * p * t:(2 * p + 1) * t], wuv_ref[2 * p])
                      + _dot(ob[(2 * p + 1) * t:(2 * p + 2) * t], wuv_ref[2 * p + 1]))
    return jnp.concatenate(pieces, axis=-1)


def _mla_prompt_body(qlat_ref, qpe_ref, ckv_ref, kpet_ref, wuv_ref, o_ref,
                     ckb_ref, ckt_ref, m_ref, l_ref, acc_ref, *, tq, tk):
    qi = pl.program_id(1)
    nkb = ckt_ref.shape[0]

    @pl.when(qi == 0)
    def _():
        ckb_ref[...] = ckv_ref[...].astype(BF16)
        for kb in range(nkb):
            ckt_ref[kb] = ckv_ref[kb * tk:(kb + 1) * tk, :].T.astype(BF16)

    ql = jnp.concatenate([qlat_ref[h] for h in range(H_A)], axis=1)
    qp = qpe_ref[...]
    feat = lax.broadcasted_iota(jnp.int32, qp.shape, 0)
    zero = jnp.zeros_like(qp)
    qp = jnp.concatenate([jnp.where((feat >= h * ROPE_A) & (feat < (h + 1) * ROPE_A), qp, zero)
                          for h in range(H_A)], axis=1)
    m_ref[...] = jnp.full(m_ref.shape, -jnp.inf, F32)
    l_ref[...] = jnp.zeros(l_ref.shape, F32)
    acc_ref[...] = jnp.zeros(acc_ref.shape, F32)

    def block(kb, masked):
        k0 = pl.multiple_of(kb * tk, tk)
        s = _dot(ckb_ref[pl.ds(k0, tk), :], ql) + _dot(kpet_ref[pl.ds(k0, tk), :], qp)
        if masked:
            key = lax.broadcasted_iota(jnp.int32, s.shape, 0) + k0
            tok = (lax.broadcasted_iota(jnp.int32, s.shape, 1) & (tq - 1)) + qi * tq
            s = jnp.where(key <= tok, s, -jnp.inf)
        m_prev = m_ref[...]
        m_new = jnp.maximum(m_prev, jnp.max(s, axis=0, keepdims=True))
        alpha = jnp.exp(m_prev - m_new)
        p = jnp.exp(s - m_new)
        l_ref[...] = alpha * l_ref[...] + jnp.sum(p, axis=0, keepdims=True)
        acc_ref[...] = alpha * acc_ref[...] + _dot(ckt_ref[kb], p.astype(BF16))
        m_ref[...] = m_new

    nfull = (qi * tq) // tk

    def full_block(kb, carry):
        block(kb, False)
        return carry

    lax.fori_loop(0, nfull, full_block, 0)
    block(nfull, True)
    ot = acc_ref[...] / l_ref[...]
    o = jnp.concatenate([ot[:, h * tq:(h + 1) * tq].T for h in range(H_A)], axis=0)
    o_ref[...] = _uv_project(o, wuv_ref, tq).astype(o_ref.dtype)


def _mla_prompt_attn(qlat, qpe, ckv, kpet, wuv, b, t):
    tq = min(256, t)
    tk = min(256, t)
    nq = t // tq
    cols = H_A * tq
    return pl.pallas_call(
        functools.partial(_mla_prompt_body, tq=tq, tk=tk),
        grid=(b, nq),
        in_specs=[pl.BlockSpec((H_A, KV_LORA, tq), lambda bi, qi: (0, 0, bi * nq + qi)),
                  pl.BlockSpec((ROPE_W, tq), lambda bi, qi: (0, bi * nq + qi)),
                  pl.BlockSpec((t, KV_LORA), lambda bi, qi: (bi, 0)),
                  pl.BlockSpec((t, ROPE_W), lambda bi, qi: (bi, 0)),
                  pl.BlockSpec(wuv.shape, lambda bi, qi: (0, 0, 0))],
        out_specs=pl.BlockSpec((tq, H_A * V_A), lambda bi, qi: (bi * nq + qi, 0)),
        out_shape=jax.ShapeDtypeStruct((b * t, H_A * V_A), BF16),
        scratch_shapes=[pltpu.VMEM((t, KV_LORA), BF16),
                        pltpu.VMEM((t // tk, KV_LORA, tk), BF16),
                        pltpu.VMEM((1, cols), F32),
                        pltpu.VMEM((1, cols), F32),
                        pltpu.VMEM((KV_LORA, cols), F32)],
        compiler_params=_cparams("parallel", "arbitrary"),
        name="mla_prompt_attn",
    )(qlat, qpe, ckv, kpet, wuv)


def _page_copies(pt_ref, first, stride, g, caches, bufs, sems, slot):
    copies = []
    for i in range(g):
        page = pt_ref[first + stride * i]
        for k, (cache, buf) in enumerate(zip(caches, bufs)):
            copies.append(pltpu.make_async_copy(cache.at[page], buf.at[slot, i], sems.at[k, slot]))
    return copies


def _paged_step(pt_ref, g, caches, bufs, sems, first_of):
    ns = pl.num_programs(1)
    lin = pl.program_id(0) * ns + pl.program_id(1)
    total = pl.num_programs(0) * ns
    slot = lin % 2

    @pl.when(lin == 0)
    def _():
        for cp in _page_copies(pt_ref, *first_of(lin), g, caches, bufs, sems, slot):
            cp.start()

    @pl.when(lin + 1 < total)
    def _():
        for cp in _page_copies(pt_ref, *first_of(lin + 1), g, caches, bufs, sems, 1 - slot):
            cp.start()

    for cp in _page_copies(pt_ref, *first_of(lin), g, caches, bufs, sems, slot):
        cp.wait()
    return slot


def _mla_sample_body(pt_ref, qlat_ref, qpe_ref, ckn_ref, kpn_ref, wuv_ref, ck_hbm, kp_hbm,
                     o_ref, ql_ref, qp_ref, ckpad_ref, kppad_ref, m_ref, l_ref, acc_ref,
                     ckbuf, kpbuf, sems, *, tn, g, nchain):
    step = pl.program_id(1)
    rows = H_A * tn
    slot = _paged_step(pt_ref, g, (ck_hbm, kp_hbm), (ckbuf, kpbuf), sems, lambda n: (n * g, 1))

    def update(s, ck, carry):
        m_prev, l_prev, acc = carry
        m_new = jnp.maximum(m_prev, jnp.max(s, axis=-1, keepdims=True))
        alpha = jnp.exp(m_prev - m_new)
        p = jnp.exp(s - m_new)
        l_new = alpha * l_prev + jnp.sum(p, axis=-1, keepdims=True)
        return m_new, l_new, alpha * acc + _dot(p.astype(BF16), ck)

    @pl.when(step == 0)
    def _():
        ql_ref[...] = qlat_ref[...].reshape(rows, KV_LORA).astype(BF16)
        qps = _head_stack(qpe_ref[...], ROPE_A, H_A).astype(BF16)
        fr = lax.broadcasted_iota(jnp.int32, (ROPE_W, ROPE_A), 0)
        fc = lax.broadcasted_iota(jnp.int32, (ROPE_W, ROPE_A), 1)
        fold = jnp.where((fr & (ROPE_A - 1)) == fc, 1.0, 0.0).astype(BF16)
        qp_ref[...] = _dot(qps, fold).astype(BF16)
        ckpad_ref[...] = jnp.zeros(ckpad_ref.shape, F32)
        kppad_ref[...] = jnp.zeros(kppad_ref.shape, F32)
        ckpad_ref[0:tn, :] = ckn_ref[...]
        kppad_ref[0:tn, :] = kpn_ref[...]
        row = lax.broadcasted_iota(jnp.int32, (rows, PAGE), 0) & (tn - 1)
        col = lax.broadcasted_iota(jnp.int32, (rows, PAGE), 1)
        ck = ckpad_ref[...].astype(BF16)
        s = _dot_nt(ql_ref[...], ck) + _dot_nt(qp_ref[...], kppad_ref[...].astype(BF16))
        s = jnp.where(col <= row, s, -jnp.inf)
        init = (jnp.full((rows, 1), -jnp.inf, F32), jnp.zeros((rows, 1), F32),
                jnp.zeros((rows, KV_LORA), F32))
        m_ref[0], l_ref[0], acc_ref[0] = update(s, ck, init)
        for c in range(1, nchain):
            m_ref[c], l_ref[c], acc_ref[c] = init

    per = g // nchain
    for c in range(nchain):
        pages = range(c * per, (c + 1) * per)
        ck = jnp.concatenate([ckbuf[slot, i] for i in pages], axis=0).astype(BF16)
        kp = jnp.concatenate([kpbuf[slot, i] for i in pages], axis=1).astype(BF16)
        s = _dot_nt(ql_ref[...], ck) + _dot(qp_ref[...], kp)
        m_ref[c], l_ref[c], acc_ref[c] = update(s, ck, (m_ref[c], l_ref[c], acc_ref[c]))

    @pl.when(step == pl.num_programs(1) - 1)
    def _():
        m = m_ref[0]
        for c in range(1, nchain):
            m = jnp.maximum(m, m_ref[c])
        l = jnp.zeros((rows, 1), F32)
        acc = jnp.zeros((rows, KV_LORA), F32)
        for c in range(nchain):
            w = jnp.exp(m_ref[c] - m)
            l = l + w * l_ref[c]
            acc = acc + w * acc_ref[c]
        o_ref[...] = _uv_project(acc / l, wuv_ref, tn).astype(o_ref.dtype)


def _mla_sample_attn(page_table, qlat, qpe, ckn, kpn, wuv, cache_ckv, cache_kpe, tn, g):
    bs, n_pages = page_table.shape
    rows = H_A * tn
    pt = page_table.reshape(-1)
    nchain = 2 if g % 2 == 0 else 1
    in_specs = [pl.BlockSpec((H_A, tn, KV_LORA), lambda b, s, pt_ref: (0, b, 0)),
                pl.BlockSpec((tn, ROPE_W), lambda b, s, pt_ref: (b, 0)),
                pl.BlockSpec((tn, KV_LORA), lambda b, s, pt_ref: (b, 0)),
                pl.BlockSpec((tn, ROPE_A), lambda b, s, pt_ref: (b, 0)),
                pl.BlockSpec(wuv.shape, lambda b, s, pt_ref: (0, 0, 0)),
                pl.BlockSpec(memory_space=pl.ANY),
                pl.BlockSpec(memory_space=pl.ANY)]
    return pl.pallas_call(
        functools.partial(_mla_sample_body, tn=tn, g=g, nchain=nchain),
        grid_spec=pltpu.PrefetchScalarGridSpec(
            num_scalar_prefetch=1,
            grid=(bs, n_pages // g),
            in_specs=in_specs,
            out_specs=pl.BlockSpec((tn, H_A * V_A), lambda b, s, pt_ref: (b, 0)),
            scratch_shapes=[pltpu.VMEM((rows, KV_LORA), BF16),
                            pltpu.VMEM((rows, ROPE_A), BF16),
                            pltpu.VMEM((PAGE, KV_LORA), F32),
                            pltpu.VMEM((PAGE, ROPE_A), F32),
                            pltpu.VMEM((nchain, rows, 1), F32),
                            pltpu.VMEM((nchain, rows, 1), F32),
                            pltpu.VMEM((nchain, rows, KV_LORA), F32),
                            pltpu.VMEM((2, g, PAGE, KV_LORA), F32),
                            pltpu.VMEM((2, g, ROPE_A, PAGE), F32),
                            pltpu.SemaphoreType.DMA((2, 2))]),
        out_shape=jax.ShapeDtypeStruct((bs * tn, H_A * V_A), F32),
        compiler_params=_cparams("arbitrary", "arbitrary"),
        name="mla_sample_attn",
    )(pt, qlat, qpe, ckn, kpn, wuv, cache_ckv, cache_kpe)


def _sb_block(qs, k, v, rest, acc, upper, mask=None):
    z = _dot_nt(qs, k)
    l_neg = _log_sigmoid(-z)
    if mask is not None:
        l_neg = jnp.where(mask, l_neg, 0.0)
    hi, lo = _split_bf16(l_neg, 2)
    later = _dot(hi, upper) + _dot(lo, upper) + rest
    a = jnp.exp(z + l_neg + later)
    if mask is not None:
        a = jnp.where(mask, a, 0.0)
    acc = acc + _dot(a.astype(BF16), v)
    rest = rest + jnp.sum(l_neg, axis=-1, keepdims=True)
    return rest, acc


def _upper_ones(n):
    r = lax.broadcasted_iota(jnp.int32, (n, n), 0)
    c = lax.broadcasted_iota(jnp.int32, (n, n), 1)
    return jnp.where(r > c, 1.0, 0.0).astype(BF16)


def _head_collapse(acc, t, width, n_heads):
    lane = lax.broadcasted_iota(jnp.int32, (t, n_heads * width), 1)
    out = jnp.zeros((t, n_heads * width), F32)
    for h in range(n_heads):
        out = out + jnp.where((lane >= h * width) & (lane < (h + 1) * width), acc[h * t:(h + 1) * t], 0.0)
    return out


def _sb_prompt_body(q_ref, k_ref, v_ref, o_ref, kb_ref, vb_ref, rest_ref, acc_ref, *, tq, tk):
    qi = pl.program_id(1)
    rows = H_D * tq

    @pl.when(qi == 0)
    def _():
        kb_ref[...] = k_ref[...].astype(BF16)
        vb_ref[...] = v_ref[...].astype(BF16)

    qs = _head_stack(q_ref[...] * SB_SCALE, DH_D, H_D).astype(BF16)
    upper = _upper_ones(tk)
    rest_ref[...] = jnp.zeros(rest_ref.shape, F32)
    acc_ref[...] = jnp.zeros(acc_ref.shape, F32)

    def block(k0, masked):
        mask = None
        if masked:
            row = lax.broadcasted_iota(jnp.int32, (rows, tk), 0) & (tq - 1)
            col = lax.broadcasted_iota(jnp.int32, (rows, tk), 1)
            mask = col + k0 < row + qi * tq
        rest, acc = _sb_block(qs, kb_ref[pl.ds(k0, tk), :], vb_ref[pl.ds(k0, tk), :],
                              rest_ref[...], acc_ref[...], upper, mask)
        rest_ref[...] = rest
        acc_ref[...] = acc

    nfull = (qi * tq) // tk
    block(pl.multiple_of(nfull * tk, tk), True)

    def full_block(i, carry):
        block(pl.multiple_of((nfull - 1 - i) * tk, tk), False)
        return carry

    lax.fori_loop(0, nfull, full_block, 0)
    o_ref[...] = _head_collapse(acc_ref[...], tq, DH_D, H_D).astype(o_ref.dtype)


def _sb_prompt_attn(q, k, v, b, t):
    tq = min(128, t)
    tk = min(256, t)
    nq = t // tq
    rows = H_D * tq
    w = H_D * DH_D
    return pl.pallas_call(
        functools.partial(_sb_prompt_body, tq=tq, tk=tk),
        grid=(b, nq),
        in_specs=[pl.BlockSpec((tq, w), lambda bi, qi: (bi * nq + qi, 0)),
                  pl.BlockSpec((t, w), lambda bi, qi: (bi, 0)),
                  pl.BlockSpec((t, w), lambda bi, qi: (bi, 0))],
        out_specs=pl.BlockSpec((tq, w), lambda bi, qi: (bi * nq + qi, 0)),
        out_shape=jax.ShapeDtypeStruct((b * t, w), BF16),
        scratch_shapes=[pltpu.VMEM((t, w), BF16),
                        pltpu.VMEM((t, w), BF16),
                        pltpu.VMEM((rows, 1), F32),
                        pltpu.VMEM((rows, w), F32)],
        compiler_params=_cparams("parallel", "arbitrary"),
        name="sb_prompt_attn",
    )(q, k, v)


def _sb_sample_body(pt_ref, q_ref, kn_ref, vn_ref, k_hbm, v_hbm, o_ref, qs_ref, kpad_ref, vpad_ref,
                    rest_ref, acc_ref, kbuf, vbuf, sems, *, tn, g, ns):
    step = pl.program_id(1)
    n_pages = ns * g

    def newest_first(n):
        seq, s = n // ns, n % ns
        return seq * n_pages + n_pages - 1 - s * g, -1

    slot = _paged_step(pt_ref, g, (k_hbm, v_hbm), (kbuf, vbuf), sems, newest_first)
    rows = H_D * tn
    r = lax.broadcasted_iota(jnp.int32, (PAGE, 2 * PAGE), 0)
    c = lax.broadcasted_iota(jnp.int32, (PAGE, 2 * PAGE), 1)
    sums = jnp.where((c >= PAGE) | (r > c), 1.0, 0.0).astype(BF16)

    def page_sums(l_negs):
        split = [_split_bf16(x, 2) for x in l_negs]
        stacked = jnp.concatenate([s[0] for s in split] + [s[1] for s in split], axis=0)
        cs = _dot(stacked, sums)
        n = len(l_negs) * rows
        cs = cs[:n] + cs[n:]
        return [(cs[i * rows:(i + 1) * rows, :PAGE], cs[i * rows:(i + 1) * rows, PAGE:])
                for i in range(len(l_negs))]

    @pl.when(step == 0)
    def _():
        qs_ref[...] = _head_stack(q_ref[...] * SB_SCALE, DH_D, H_D).astype(BF16)
        kpad_ref[...] = jnp.zeros(kpad_ref.shape, F32)
        vpad_ref[...] = jnp.zeros(vpad_ref.shape, F32)
        kpad_ref[0:tn, :] = kn_ref[...]
        vpad_ref[0:tn, :] = vn_ref[...]
        row = lax.broadcasted_iota(jnp.int32, (rows, PAGE), 0) & (tn - 1)
        col = lax.broadcasted_iota(jnp.int32, (rows, PAGE), 1)
        mask = col < row
        z = _dot_nt(qs_ref[...], kpad_ref[...].astype(BF16))
        l_neg = jnp.where(mask, _log_sigmoid(-z), 0.0)
        (later, total), = page_sums([l_neg])
        a = jnp.where(mask, jnp.exp(z + l_neg + later), 0.0)
        acc_ref[...] = _dot(a.astype(BF16), vpad_ref[...].astype(BF16))
        rest_ref[...] = total

    kt = jnp.concatenate([kbuf[slot, i] for i in range(g)], axis=1).astype(BF16)
    vt = jnp.concatenate([vbuf[slot, i] for i in range(g)], axis=1).astype(BF16)
    z = _dot(qs_ref[...], kt)
    l_neg = _log_sigmoid(-z)
    zl = z + l_neg
    stick = rest_ref[...]
    logits = []
    for i, (later, total) in enumerate(page_sums([l_neg[:, i * PAGE:(i + 1) * PAGE] for i in range(g)])):
        logits.append(zl[:, i * PAGE:(i + 1) * PAGE] + later + stick)
        stick = stick + total
    a = jnp.exp(jnp.concatenate(logits, axis=1)).astype(BF16)
    acc_ref[...] += _dot_nt(a, vt)
    rest_ref[...] = stick

    @pl.when(step == pl.num_programs(1) - 1)
    def _():
        o_ref[...] = _head_collapse(acc_ref[...], tn, DH_D, H_D).astype(o_ref.dtype)


def _sb_sample_attn(page_table, q, kn, vn, cache_k, cache_v, tn, g):
    bs, n_pages = page_table.shape
    rows = H_D * tn
    w = H_D * DH_D
    pt = page_table.reshape(-1)

    in_specs = [pl.BlockSpec((tn, w), lambda b, s, pt_ref: (b, 0))] * 3
    in_specs += [pl.BlockSpec(memory_space=pl.ANY)] * 2
    return pl.pallas_call(
        functools.partial(_sb_sample_body, tn=tn, g=g, ns=n_pages // g),
        grid_spec=pltpu.PrefetchScalarGridSpec(
            num_scalar_prefetch=1,
            grid=(bs, n_pages // g),
            in_specs=in_specs,
            out_specs=pl.BlockSpec((tn, w), lambda b, s, pt_ref: (b, 0)),
            scratch_shapes=[pltpu.VMEM((rows, w), BF16),
                            pltpu.VMEM((PAGE, w), F32),
                            pltpu.VMEM((PAGE, w), F32),
                            pltpu.VMEM((rows, PAGE), F32),
                            pltpu.VMEM((rows, w), F32),
                            pltpu.VMEM((2, g, w, PAGE), F32),
                            pltpu.VMEM((2, g, w, PAGE), F32),
                            pltpu.SemaphoreType.DMA((2, 2))]),
        out_shape=jax.ShapeDtypeStruct((bs * tn, w), F32),
        compiler_params=_cparams("arbitrary", "arbitrary"),
        name="sb_sample_attn",
    )(pt, q, kn, vn, cache_k, cache_v)


def _gated_body(*refs, mode, lq, nblk, ngrp, dk, dv, dkp, dvp, has_s0):
    nh = 4
    nsq = CHUNK // lq
    it = iter(refs)
    if mode == "hgrn":
        zq_ref, zf_ref, zv_ref, zg_ref, lb_ref, nw_ref = (next(it) for _ in range(6))
    else:
        zq_ref, zk_ref, zv_ref, zg_ref, ga_ref, wg_hi_ref, wg_lo_ref, bg_ref, nw_ref = (next(it) for _ in range(9))
    s0_ref = next(it) if has_s0 else None
    o_ref, sout_ref, s_ref = next(it), next(it), next(it)
    step = pl.program_id(1)

    if has_s0:
        s_ref[...] = jnp.zeros(s_ref.shape, F32)
        s_ref[:, :, 0:dk, 0:dv] = s0_ref[...]
    else:
        @pl.when(step == 0)
        def _():
            s_ref[...] = jnp.zeros(s_ref.shape, F32)

    r = lax.broadcasted_iota(jnp.int32, (CHUNK, CHUNK), 0)
    c = lax.broadcasted_iota(jnp.int32, (CHUNK, CHUNK), 1)
    seq_r, seq_c = r // lq, c // lq
    same = seq_r == seq_c
    causal = same & (r >= c)
    ref_pos = seq_r * lq + lq // 2
    ones = lambda m: jnp.where(m, 1.0, 0.0).astype(BF16)
    sums = jnp.concatenate([ones(causal), ones(same & (c <= ref_pos)), ones(same)], axis=0)
    row_seq = lax.broadcasted_iota(jnp.int32, (CHUNK, dkp), 0) // lq

    def chunk(gi, ci):
        rows = pl.ds(pl.multiple_of(ci * CHUNK, CHUNK), CHUNK)
        if mode == "hgrn":
            lb = lb_ref[...]
            f = lb + (1.0 - lb) * _sigmoid(zf_ref[gi, rows, :])
            logf, k = jnp.log(f), 1.0 - f
            zq = zq_ref[gi, rows, :]
            q = zq * _sigmoid(zq)
        else:
            ga_hi, ga_lo = _split_bf16(ga_ref[gi, rows, :], 2)
            logits = (_dot(ga_hi, wg_hi_ref[...]) + _dot(ga_hi, wg_lo_ref[...])
                      + _dot(ga_lo, wg_hi_ref[...]) + bg_ref[...])
            logf = _log_sigmoid(logits) * (1.0 / GATE_NORM)
            k = zk_ref[gi, rows, :]
            q = zq_ref[gi, rows, :] * (dk ** -0.5)
        v = zv_ref[gi, rows, :].astype(BF16)
        parts = _split_bf16(logf, 3)
        gall = _dot(sums, parts[0]) + _dot(sums, parts[1]) + _dot(sums, parts[2])
        g, gr, gl = gall[:CHUNK], gall[CHUNK:2 * CHUNK], gall[2 * CHUNK:]
        qg = (q * jnp.exp(g - gr)).astype(BF16)
        kg = (k * jnp.exp(gr - g)).astype(BF16)
        qs = q * jnp.exp(g)
        kd = k * jnp.exp(gl - g)
        el = jnp.exp(gl)
        gate = zg_ref[gi, rows, :]
        outs = []
        for h in range(nh):
            ks, vs = slice(h * dkp, (h + 1) * dkp), slice(h * dvp, (h + 1) * dvp)
            a = jnp.where(causal, _dot_nt(qg[:, ks], kg[:, ks]), 0.0)
            o = _dot(a.astype(BF16), v[:, vs])
            for s in range(nsq):
                if nsq == 1:
                    qs_s, kd_s = qs[:, ks], kd[:, ks]
                else:
                    qs_s = jnp.where(row_seq == s, qs[:, ks], 0.0)
                    kd_s = jnp.where(row_seq == s, kd[:, ks], 0.0)
                st = s_ref[gi * nsq + s, h]
                o = o + _dot(qs_s.astype(BF16), st.astype(BF16))
                e_col = jnp.broadcast_to(el[s * lq:s * lq + 1, ks], (dkp, dkp)).T
                if dvp > dkp:
                    e_col = jnp.concatenate([e_col] * (dvp // dkp), axis=1)
                s_ref[gi * nsq + s, h] = e_col * st + _dot_tn(kd_s.astype(BF16), v[:, vs])
            nwh = nw_ref[...]
            gh = gate[:, vs]
            outs.append(_rms(o, nwh, dv) * (gh * _sigmoid(gh)))
        o_ref[gi, rows, :] = jnp.concatenate(outs, axis=-1).astype(o_ref.dtype)

    if nblk == 1:
        for gi in range(ngrp):
            chunk(gi, 0)
    else:
        def loop_body(ci, carry):
            for gi in range(ngrp):
                chunk(gi, ci)
            return carry
        lax.fori_loop(0, nblk, loop_body, 0)

    if has_s0:
        sout_ref[...] = s_ref[:, :, 0:dk, 0:dv]
    else:
        @pl.when(step == pl.num_programs(1) - 1)
        def _():
            sout_ref[...] = s_ref[:, :, 0:dk, 0:dv]


def _gated_linear(mode, arrays, col_blocks, params, s0, b, t, dk, dv, dkp, dvp):
    nh = 4
    m = b * t
    lq = min(CHUNK, t)
    nsq = CHUNK // lq
    if s0 is None:
        n_groups, group_rows = b, t
        nblk = min(8, t // CHUNK)
    else:
        n_groups, group_rows = m // CHUNK, CHUNK
        nblk = 1
    ngrp = 2 if n_groups % 2 == 0 else 1
    tr = CHUNK * nblk
    grid = (n_groups // ngrp, group_rows // tr)
    state_map = lambda bi, si: (bi, 0, 0, 0)
    arrays = [a.reshape(n_groups, group_rows, a.shape[-1]) for a in arrays]
    in_specs = [pl.BlockSpec((ngrp, tr, w), functools.partial(lambda bi, si, cb: (bi, si, cb), cb=cb))
                for (w, cb) in col_blocks]
    in_specs += [pl.BlockSpec(p.shape, lambda bi, si: (0, 0)) for p in params]
    operands = list(arrays) + list(params)
    if s0 is not None:
        in_specs.append(pl.BlockSpec((ngrp * nsq, nh, dk, dv), state_map))
        operands.append(s0)
    o, s_out = pl.pallas_call(
        functools.partial(_gated_body, mode=mode, lq=lq, nblk=nblk, ngrp=ngrp, dk=dk, dv=dv, dkp=dkp,
                          dvp=dvp, has_s0=s0 is not None),
        grid=grid,
        in_specs=in_specs,
        out_specs=[pl.BlockSpec((ngrp, tr, nh * dvp), lambda bi, si: (bi, si, 0)),
                   pl.BlockSpec((ngrp * nsq, nh, dk, dv), state_map)],
        out_shape=[jax.ShapeDtypeStruct((n_groups, group_rows, nh * dvp), BF16),
                   jax.ShapeDtypeStruct((b, nh, dk, dv), F32)],
        scratch_shapes=[pltpu.VMEM((ngrp * nsq, nh, dkp, dvp), F32)],
        compiler_params=_cparams("parallel", "arbitrary"),
        name="gated_linear_" + mode,
    )(*operands)
    return o.reshape(m, nh * dvp), s_out


def _ffn_body(x_ref, oa_ref, ob_ref, woa_ref, wob_ref, fnw_ref, wg_ref, wu_ref, wo_ref, finw_ref,
              y_ref, h_ref, xn_ref, acc_ref, *, final_norm):
    j = pl.program_id(1)

    @pl.when(j == 0)
    def _():
        h = (x_ref[...] + _dot(oa_ref[...].astype(BF16), woa_ref[...])
             + _dot(ob_ref[...].astype(BF16), wob_ref[...]))
        h_ref[...] = h
        xn_ref[...] = _rms(h, fnw_ref[...]).astype(BF16)
        acc_ref[...] = jnp.zeros(acc_ref.shape, F32)

    xn = xn_ref[...]
    gate = _dot(xn, wg_ref[...])
    up = _dot(xn, wu_ref[...])
    act = (gate * _sigmoid(gate) * up).astype(BF16)
    acc_ref[...] += _dot(act, wo_ref[...])

    @pl.when(j == pl.num_programs(1) - 1)
    def _():
        y = h_ref[...] + acc_ref[...]
        if final_norm:
            y = _rms(y, finw_ref[...])
        y_ref[...] = y


def _mix_ffn(x, oa, ob, woa, wob, fnw, w_in, w_out, finw, final_norm):
    m = x.shape[0]
    tm = min(512, m)
    tf = D_FF // 2
    nf = D_FF // tf
    ka, kb = oa.shape[1], ob.shape[1]
    return pl.pallas_call(
        functools.partial(_ffn_body, final_norm=final_norm),
        grid=(m // tm, nf),
        in_specs=[pl.BlockSpec((tm, D_MODEL), lambda i, j: (i, 0)),
                  pl.BlockSpec((tm, ka), lambda i, j: (i, 0)),
                  pl.BlockSpec((tm, kb), lambda i, j: (i, 0)),
                  pl.BlockSpec((ka, D_MODEL), lambda i, j: (0, 0)),
                  pl.BlockSpec((kb, D_MODEL), lambda i, j: (0, 0)),
                  pl.BlockSpec((1, D_MODEL), lambda i, j: (0, 0)),
                  pl.BlockSpec((D_MODEL, tf), lambda i, j: (0, j)),
                  pl.BlockSpec((D_MODEL, tf), lambda i, j: (0, j + nf)),
                  pl.BlockSpec((tf, D_MODEL), lambda i, j: (j, 0)),
                  pl.BlockSpec((1, D_MODEL), lambda i, j: (0, 0))],
        out_specs=pl.BlockSpec((tm, D_MODEL), lambda i, j: (i, 0)),
        out_shape=jax.ShapeDtypeStruct((m, D_MODEL), F32),
        scratch_shapes=[pltpu.VMEM((tm, D_MODEL), F32),
                        pltpu.VMEM((tm, D_MODEL), BF16),
                        pltpu.VMEM((tm, D_MODEL), F32)],
        compiler_params=_cparams("parallel", "arbitrary"),
        name="mix_ffn",
    )(x, oa, ob, woa, wob, fnw.reshape(1, -1), w_in, w_in, w_out, finw.reshape(1, -1))


def _pad_heads(w, nh, d, dp, axis):
    shape = w.shape
    w = w.reshape(shape[:axis] + (nh, d) + shape[axis + 1:])
    pad = [(0, 0)] * w.ndim
    pad[axis + 1] = (0, dp - d)
    w = jnp.pad(w, pad)
    return w.reshape(shape[:axis] + (nh * dp,) + shape[axis + 1:])


def _rope_tables(pos):
    inv = ROPE_THETA ** (-jnp.arange(0, ROPE_A, 2, dtype=F32) / ROPE_A)
    ang = pos[:, None] * inv[None, :]
    cos = jnp.repeat(jnp.cos(ang), 2, axis=1)
    sin = jnp.repeat(jnp.sin(ang), 2, axis=1) * jnp.tile(jnp.array([-1.0, 1.0], F32), ROPE_A // 2)[None, :]
    return jnp.tile(cos, (1, H_A)), jnp.tile(sin, (1, H_A))


def _prepare_weights(P):
    W = {}
    w_ab = P["w_in_ab"][0]
    o_ckv, o_kpe, o_h = Q_LORA, Q_LORA + KV_LORA, Q_LORA + KV_LORA + ROPE_A
    W["w_mla"] = jnp.concatenate(
        [w_ab[:, :o_kpe], jnp.tile(w_ab[:, o_kpe:o_h], (1, H_A))], axis=1).astype(BF16)
    W["w_hgrn"] = w_ab[:, o_h:].astype(BF16)
    wuq = P["mla_w_uq"][0].reshape(Q_LORA, H_A, NOPE_A + ROPE_A)
    W["w_uq"] = jnp.concatenate([wuq[:, :, :NOPE_A].reshape(Q_LORA, -1),
                                 wuq[:, :, NOPE_A:].reshape(Q_LORA, -1)], axis=1).astype(BF16)
    half = (jnp.arange(H_A) % 2)[:, None, None]
    wk = jnp.transpose(P["mla_w_uk"][0], (1, 2, 0))
    W["w_uk"] = jnp.where(half == 0, jnp.pad(wk, ((0, 0), (0, NOPE_A), (0, 0))),
                          jnp.pad(wk, ((0, 0), (NOPE_A, 0), (0, 0)))).astype(BF16)
    W["w_uk_t"] = jnp.transpose(W["w_uk"], (0, 2, 1))
    wv = jnp.transpose(P["mla_w_uv"][0], (1, 0, 2))
    W["w_uv"] = jnp.where(half == 0, jnp.pad(wv, ((0, 0), (0, 0), (0, V_A))),
                          jnp.pad(wv, ((0, 0), (0, 0), (V_A, 0)))).astype(BF16)
    lb = jnp.cumsum(jax.nn.softmax(P["hgrn_lb_logits"].astype(F32), axis=0), axis=0)[0]
    W["hgrn_lb"] = lb.reshape(1, -1)
    W["hgrn_nw"] = P["hgrn_norm_w"][0].reshape(1, -1)
    w_out_ab = P["w_out_ab"][0].astype(BF16)
    W["w_out_a"], W["w_out_b"] = w_out_ab[:H_A * V_A], w_out_ab[H_A * V_A:]

    w_cd = P["w_in_cd"][0]
    offs = [0]
    for wdt in (H_C * DK_C, H_C * DK_C, H_C * DV_C, H_C * DV_C, GATE_RANK, H_D * DH_D * 3):
        offs.append(offs[-1] + wdt)
    seg = lambda i: w_cd[:, offs[i]:offs[i + 1]]
    W["w_gla"] = jnp.concatenate(
        [_pad_heads(seg(0), H_C, DK_C, DKP_C, 1), _pad_heads(seg(1), H_C, DK_C, DKP_C, 1),
         _pad_heads(seg(2), H_C, DV_C, DVP_C, 1), _pad_heads(seg(3), H_C, DV_C, DVP_C, 1),
         jnp.pad(seg(4), ((0, 0), (0, GATE_RANK_P - GATE_RANK)))], axis=1).astype(BF16)
    W["w_sb"] = seg(5).astype(BF16)
    wg2 = jnp.pad(_pad_heads(P["gla_w_gate2"][0], H_C, DK_C, DKP_C, 1), ((0, GATE_RANK_P - GATE_RANK), (0, 0)))
    W["wg2_hi"] = wg2.astype(BF16)
    W["wg2_lo"] = (wg2 - W["wg2_hi"].astype(F32)).astype(BF16)
    W["bg2"] = _pad_heads(P["gla_b_gate2"][0].reshape(1, -1), H_C, DK_C, DKP_C, 1)
    W["gla_nw"] = jnp.pad(P["gla_norm_w"][0], (0, DVP_C - DV_C)).reshape(1, -1)
    w_out_cd = P["w_out_cd"][0]
    W["w_out_c"] = _pad_heads(w_out_cd[:H_C * DV_C], H_C, DV_C, DVP_C, 0).astype(BF16)
    W["w_out_d"] = w_out_cd[H_C * DV_C:].astype(BF16)
    W["w_ffn_in"] = [P["w_ffn_in"][l].astype(BF16) for l in range(2)]
    W["w_ffn_out"] = [P["w_ffn_out"][l].astype(BF16) for l in range(2)]
    return W


def _trunk(x3, pos, P, W, ctx):
    b, t, _ = x3.shape
    m = b * t
    x = x3.reshape(m, D_MODEL)
    sample = ctx is not None
    cos, sin = _rope_tables(pos)
    reps = min(512, m) // t
    if reps > 1:
        cos, sin = jnp.tile(cos, (reps, 1)), jnp.tile(sin, (reps, 1))

    qlat, qpe, ckv, kpe, kpet = _mla_prep(
        x, P["attn_norm_w"][0], W["w_mla"], P["mla_q_norm_w"][0], W["w_uq"],
        W["w_uk"] if sample else W["w_uk_t"], P["mla_kv_norm_w"][0], cos, sin,
        F32 if sample else BF16, transposed_q=not sample)
    zh = _norm_matmul(x, P["attn_norm_w"][0], W["w_hgrn"], H_B * DK_B, split=True)
    if sample:
        o_a = _mla_sample_attn(ctx["page_table"], qlat, qpe, ckv, kpe, W["w_uv"],
                               ctx["cache_mla_ckv"], ctx["cache_mla_kpe"], t, ctx["pages_per_step"])
        s0 = ctx["state_hgrn"][0]
    else:
        o_a = _mla_prompt_attn(qlat, qpe, ckv, kpet, W["w_uv"], b, t)
        s0 = None
    wb = H_B * DK_B
    o_b, s_hgrn = _gated_linear("hgrn", zh, [(wb, 0)] * 4, [W["hgrn_lb"], W["hgrn_nw"]], s0,
                                b, t, DK_B, DV_B, DK_B, DV_B)
    h1 = _mix_ffn(x, o_a, o_b, W["w_out_a"], W["w_out_b"], P["ffn_norm_w"][0],
                  W["w_ffn_in"][0], W["w_ffn_out"][0], P["final_norm_w"], final_norm=False)

    zg = _norm_matmul(h1, P["attn_norm_w"][1], W["w_gla"], 640, split=False)
    sq, sk, sv = _norm_matmul(h1, P["attn_norm_w"][1], W["w_sb"], H_D * DH_D, split=True)
    if sample:
        o_d = _sb_sample_attn(ctx["page_table"], sq, sk, sv, ctx["cache_sb_k"], ctx["cache_sb_v"],
                              t, ctx["pages_per_step"])
        s0 = ctx["state_gla"][0]
    else:
        o_d = _sb_prompt_attn(sq, sk, sv, b, t)
        s0 = None
    wq, wv = H_C * DKP_C, H_C * DVP_C
    o_c, s_gla = _gated_linear(
        "gla", [zg] * 5, [(wq, 0), (wq, 1), (wv, 1), (wv, 2), (GATE_RANK_P, (2 * wq + 2 * wv) // GATE_RANK_P)],
        [W["wg2_hi"], W["wg2_lo"], W["bg2"], W["gla_nw"]], s0, b, t, DK_C, DV_C, DKP_C, DVP_C)
    y = _mix_ffn(h1, o_c, o_d, W["w_out_c"], W["w_out_d"], P["ffn_norm_w"][1],
                 W["w_ffn_in"][1], W["w_ffn_out"][1], P["final_norm_w"], final_norm=True)

    new = (ckv.reshape(1, b, t, KV_LORA), kpe.reshape(1, b, t, ROPE_A), s_hgrn[None],
           sk.reshape(1, b, t, H_D, DH_D), sv.reshape(1, b, t, H_D, DH_D), s_gla[None])
    return y.reshape(b, t, D_MODEL), new


def kernel(x_prompt, x_sample, cache_mla_ckv, cache_mla_kpe, state_hgrn, cache_sb_k, cache_sb_v, state_gla, page_table, attn_norm_w, ffn_norm_w, final_norm_w, w_in_ab, mla_q_norm_w, mla_w_uq, mla_kv_norm_w, mla_w_uk, mla_w_uv, hgrn_lb_logits, hgrn_norm_w, w_out_ab, w_in_cd, gla_w_gate2, gla_b_gate2, gla_norm_w, w_out_cd, w_ffn_in, w_ffn_out):
    P = {
        "attn_norm_w": attn_norm_w, "ffn_norm_w": ffn_norm_w, "final_norm_w": final_norm_w,
        "w_in_ab": w_in_ab, "mla_q_norm_w": mla_q_norm_w, "mla_w_uq": mla_w_uq,
        "mla_kv_norm_w": mla_kv_norm_w, "mla_w_uk": mla_w_uk, "mla_w_uv": mla_w_uv,
        "hgrn_lb_logits": hgrn_lb_logits, "hgrn_norm_w": hgrn_norm_w, "w_out_ab": w_out_ab,
        "w_in_cd": w_in_cd, "gla_w_gate2": gla_w_gate2, "gla_b_gate2": gla_b_gate2,
        "gla_norm_w": gla_norm_w, "w_out_cd": w_out_cd, "w_ffn_in": w_ffn_in, "w_ffn_out": w_ffn_out,
    }
    W = _prepare_weights(P)
    n_pages = page_table.shape[1]
    past = n_pages * PAGE
    pos_prompt = jnp.arange(x_prompt.shape[1], dtype=F32)
    pos_sample = jnp.arange(x_sample.shape[1], dtype=F32) + float(past)
    n_pool = cache_sb_k.shape[1]
    ctx = {
        "page_table": page_table,
        "cache_mla_ckv": cache_mla_ckv[0],
        "cache_mla_kpe": jnp.transpose(cache_mla_kpe[0], (0, 2, 1)),
        "state_hgrn": state_hgrn,
        "cache_sb_k": jnp.transpose(cache_sb_k[0], (0, 2, 3, 1)).reshape(n_pool, H_D * DH_D, PAGE),
        "cache_sb_v": jnp.transpose(cache_sb_v[0], (0, 2, 3, 1)).reshape(n_pool, H_D * DH_D, PAGE),
        "state_gla": state_gla,
        "pages_per_step": min(32, n_pages),
    }
    y_p, new_p = _trunk(x_prompt, pos_prompt, P, W, None)
    y_s, new_s = _trunk(x_sample, pos_sample, P, W, ctx)
    return (y_p, y_s) + new_p + new_s
```

```python
import functools

import jax
import jax.numpy as jnp
from jax import lax
from jax.experimental import pallas as pl
from jax.experimental.pallas import tpu as pltpu

F32 = jnp.float32
BF16 = jnp.bfloat16

D_MODEL = 1024
PAGE = 128
H_A, NOPE_A, ROPE_A, V_A = 8, 64, 32, 64
Q_LORA, KV_LORA = 384, 256
ROPE_THETA = 10000.0
MLA_SCALE = (NOPE_A + ROPE_A) ** -0.5
H_B, DK_B, DV_B = 4, 128, 128
H_C, DK_C, DV_C = 4, 96, 192
DKP_C, DVP_C = 128, 256
GATE_RANK, GATE_RANK_P = 16, 128
GATE_NORM = 16.0
H_D, DH_D = 4, 64
SB_SCALE = DH_D ** -0.5
D_FF = 2816
EPS = 1e-6
CHUNK = 64
ROPE_W = H_A * ROPE_A

VMEM_LIMIT = 56 * 1024 * 1024


def _cparams(*sem):
    return pltpu.CompilerParams(dimension_semantics=sem, vmem_limit_bytes=VMEM_LIMIT)


def _dot(a, b):
    return jnp.dot(a, b, preferred_element_type=F32)


def _dot_nt(a, b):
    return lax.dot_general(a, b, (((1,), (1,)), ((), ())), preferred_element_type=F32)


def _dot_tn(a, b):
    return lax.dot_general(a, b, (((0,), (0,)), ((), ())), preferred_element_type=F32)


def _split_bf16(x, n):
    parts = []
    for _ in range(n - 1):
        p = x.astype(BF16)
        parts.append(p)
        x = x - p.astype(F32)
    parts.append(x.astype(BF16))
    return parts


def _rms(x, w, n=None):
    n = x.shape[-1] if n is None else n
    ms = jnp.sum(x * x, axis=-1, keepdims=True) * (1.0 / n)
    return x * lax.rsqrt(ms + EPS) * w


def _sigmoid(x):
    return 1.0 / (1.0 + jnp.exp(-x))


def _log_sigmoid(x):
    return jnp.minimum(x, 0.0) - jnp.log(1.0 + jnp.exp(-jnp.abs(x)))


def _rope(x, cos, sin_signed):
    n = x.shape[-1]
    lane = lax.broadcasted_iota(jnp.int32, x.shape, x.ndim - 1)
    nxt = pltpu.roll(x, n - 1, x.ndim - 1)
    prv = pltpu.roll(x, 1, x.ndim - 1)
    swap = jnp.where((lane & 1) == 0, nxt, prv)
    return x * cos + swap * sin_signed


def _norm_matmul_body(x_ref, nw_ref, w_ref, *rest, n_out):
    outs, xn_ref = rest[:n_out], rest[n_out]
    j = pl.program_id(1)

    @pl.when(j == 0)
    def _():
        xn_ref[...] = _rms(x_ref[...], nw_ref[...]).astype(BF16)

    z = _dot(xn_ref[...], w_ref[...])
    if n_out == 1:
        outs[0][...] = z
    else:
        for k in range(n_out):
            @pl.when(j == k)
            def _(k=k):
                outs[k][...] = z


def _norm_matmul(x, nw, w, tn, split):
    m, kdim = x.shape
    n = w.shape[1]
    tm = min(1024, m)
    nj = n // tn
    n_out = nj if split else 1
    if split:
        out_shape = [jax.ShapeDtypeStruct((m, tn), F32)] * nj
        out_specs = [pl.BlockSpec((tm, tn), lambda i, j: (i, 0))] * nj
    else:
        out_shape = [jax.ShapeDtypeStruct((m, n), F32)]
        out_specs = [pl.BlockSpec((tm, tn), lambda i, j: (i, j))]
    outs = pl.pallas_call(
        functools.partial(_norm_matmul_body, n_out=n_out),
        grid=(m // tm, nj),
        in_specs=[pl.BlockSpec((tm, kdim), lambda i, j: (i, 0)),
                  pl.BlockSpec((1, kdim), lambda i, j: (0, 0)),
                  pl.BlockSpec((kdim, tn), lambda i, j: (0, j))],
        out_specs=out_specs,
        out_shape=out_shape,
        scratch_shapes=[pltpu.VMEM((tm, kdim), BF16)],
        compiler_params=_cparams("parallel", "arbitrary"),
        name="norm_matmul",
    )(x, nw.reshape(1, kdim), w)
    return outs if split else outs[0]


def _mla_prep_body(x_ref, anw_ref, w_ref, qnw_ref, wuq_ref, wk_ref, kvnw_ref, cos_ref, sin_ref,
                   qlat_ref, qpe_ref, ckv_ref, kpe_ref, kpet_ref, *, transposed_q):
    xn = _rms(x_ref[...], anw_ref[...]).astype(BF16)
    z = _dot(xn, w_ref[...])
    cqn = _rms(z[:, :Q_LORA], qnw_ref[...]).astype(BF16)
    q = _dot(cqn, wuq_ref[...])
    qn = q[:, :H_A * NOPE_A].astype(BF16)
    for h in range(H_A):
        qpair = qn[:, 128 * (h // 2):128 * (h // 2 + 1)]
        ql = _dot_nt(wk_ref[h], qpair) if transposed_q else _dot(qpair, wk_ref[h])
        qlat_ref[h] = (ql * MLA_SCALE).astype(qlat_ref.dtype)
    cos, sin = cos_ref[...], sin_ref[...]
    qr = _rope(q[:, H_A * NOPE_A:], cos, sin) * MLA_SCALE
    qpe_ref[...] = (qr.T if transposed_q else qr).astype(qpe_ref.dtype)
    ckv_ref[...] = _rms(z[:, Q_LORA:Q_LORA + KV_LORA], kvnw_ref[...])
    kr = _rope(z[:, Q_LORA + KV_LORA:], cos, sin)
    kpet_ref[...] = kr.astype(BF16)
    kpe_ref[...] = kr[:, :ROPE_A]


def _mla_prep(x, anw, w_mla, qnw, wuq, wk, kvnw, cos, sin, q_dtype, transposed_q):
    m = x.shape[0]
    tm = min(512, m)
    nt = cos.shape[0] // tm
    wz = w_mla.shape[1]
    if transposed_q:
        q_specs = [pl.BlockSpec((H_A, KV_LORA, tm), lambda i: (0, 0, i)),
                   pl.BlockSpec((ROPE_W, tm), lambda i: (0, i))]
        q_shapes = [jax.ShapeDtypeStruct((H_A, KV_LORA, m), q_dtype),
                    jax.ShapeDtypeStruct((ROPE_W, m), q_dtype)]
    else:
        q_specs = [pl.BlockSpec((H_A, tm, KV_LORA), lambda i: (0, i, 0)),
                   pl.BlockSpec((tm, ROPE_W), lambda i: (i, 0))]
        q_shapes = [jax.ShapeDtypeStruct((H_A, m, KV_LORA), q_dtype),
                    jax.ShapeDtypeStruct((m, ROPE_W), q_dtype)]
    return pl.pallas_call(
        functools.partial(_mla_prep_body, transposed_q=transposed_q),
        grid=(m // tm,),
        in_specs=[pl.BlockSpec((tm, D_MODEL), lambda i: (i, 0)),
                  pl.BlockSpec((1, D_MODEL), lambda i: (0, 0)),
                  pl.BlockSpec((D_MODEL, wz), lambda i: (0, 0)),
                  pl.BlockSpec((1, Q_LORA), lambda i: (0, 0)),
                  pl.BlockSpec(wuq.shape, lambda i: (0, 0)),
                  pl.BlockSpec(wk.shape, lambda i: (0, 0, 0)),
                  pl.BlockSpec((1, KV_LORA), lambda i: (0, 0)),
                  pl.BlockSpec((tm, ROPE_W), lambda i: (i % nt, 0)),
                  pl.BlockSpec((tm, ROPE_W), lambda i: (i % nt, 0))],
        out_specs=q_specs + [
                   pl.BlockSpec((tm, KV_LORA), lambda i: (i, 0)),
                   pl.BlockSpec((tm, ROPE_A), lambda i: (i, 0)),
                   pl.BlockSpec((tm, ROPE_W), lambda i: (i, 0))],
        out_shape=q_shapes + [
                   jax.ShapeDtypeStruct((m, KV_LORA), F32),
                   jax.ShapeDtypeStruct((m, ROPE_A), F32),
                   jax.ShapeDtypeStruct((m, ROPE_W), BF16)],
        compiler_params=_cparams("parallel"),
        name="mla_prep",
    )(x, anw.reshape(1, -1), w_mla, qnw.reshape(1, -1), wuq, wk, kvnw.reshape(1, -1), cos, sin)


def _head_stack(x, width, n_heads):
    lane = lax.broadcasted_iota(jnp.int32, x.shape, 1)
    zero = jnp.zeros_like(x)
    return jnp.concatenate(
        [jnp.where((lane >= h * width) & (lane < (h + 1) * width), x, zero) for h in range(n_heads)], axis=0)


def _uv_project(o, wuv_ref, t):
    ob = o.astype(BF16)
    pieces = []
    for p in range(H_A // 2):
        pieces.append(_dot(ob[2 * p * t:(2 * p + 1) * t], wuv_ref[2 * p])
                      + _dot(ob[(2 * p + 1) * t:(2 * p + 2) * t], wuv_ref[2 * p + 1]))
    return jnp.concatenate(pieces, axis=-1)


def _mla_prompt_body(qlat_ref, qpe_ref, ckv_ref, kpet_ref, wuv_ref, o_ref,
                     ckb_ref, ckt_ref, m_ref, l_ref, acc_ref, *, tq, tk):
    qi = pl.program_id(1)
    nkb = ckt_ref.shape[0]

    @pl.when(qi == 0)
    def _():
        ckb_ref[...] = ckv_ref[...].astype(BF16)
        for kb in range(nkb):
            ckt_ref[kb] = ckv_ref[kb * tk:(kb + 1) * tk, :].T.astype(BF16)

    ql = jnp.concatenate([qlat_ref[h] for h in range(H_A)], axis=1)
    qp = qpe_ref[...]
    feat = lax.broadcasted_iota(jnp.int32, qp.shape, 0)
    zero = jnp.zeros_like(qp)
    qp = jnp.concatenate([jnp.where((feat >= h * ROPE_A) & (feat < (h + 1) * ROPE_A), qp, zero)
                          for h in range(H_A)], axis=1)
    m_ref[...] = jnp.full(m_ref.shape, -jnp.inf, F32)
    l_ref[...] = jnp.zeros(l_ref.shape, F32)
    acc_ref[...] = jnp.zeros(acc_ref.shape, F32)

    def block(kb, masked):
        k0 = pl.multiple_of(kb * tk, tk)
        s = _dot(ckb_ref[pl.ds(k0, tk), :], ql) + _dot(kpet_ref[pl.ds(k0, tk), :], qp)
        if masked:
            key = lax.broadcasted_iota(jnp.int32, s.shape, 0) + k0
            tok = (lax.broadcasted_iota(jnp.int32, s.shape, 1) & (tq - 1)) + qi * tq
            s = jnp.where(key <= tok, s, -jnp.inf)
        m_prev = m_ref[...]
        m_new = jnp.maximum(m_prev, jnp.max(s, axis=0, keepdims=True))
        alpha = jnp.exp(m_prev - m_new)
        p = jnp.exp(s - m_new)
        l_ref[...] = alpha * l_ref[...] + jnp.sum(p, axis=0, keepdims=True)
        acc_ref[...] = alpha * acc_ref[...] + _dot(ckt_ref[kb], p.astype(BF16))
        m_ref[...] = m_new

    nfull = (qi * tq) // tk

    def full_block(kb, carry):
        block(kb, False)
        return carry

    lax.fori_loop(0, nfull, full_block, 0)
    block(nfull, True)
    ot = acc_ref[...] / l_ref[...]
    o = jnp.concatenate([ot[:, h * tq:(h + 1) * tq].T for h in range(H_A)], axis=0)
    o_ref[...] = _uv_project(o, wuv_ref, tq).astype(o_ref.dtype)


def _mla_prompt_attn(qlat, qpe, ckv, kpet, wuv, b, t):
    tq = min(256, t)
    tk = min(256, t)
    nq = t // tq
    cols = H_A * tq
    return pl.pallas_call(
        functools.partial(_mla_prompt_body, tq=tq, tk=tk),
        grid=(b, nq),
        in_specs=[pl.BlockSpec((H_A, KV_LORA, tq), lambda bi, qi: (0, 0, bi * nq + qi)),
                  pl.BlockSpec((ROPE_W, tq), lambda bi, qi: (0, bi * nq + qi)),
                  pl.BlockSpec((t, KV_LORA), lambda bi, qi: (bi, 0)),
                  pl.BlockSpec((t, ROPE_W), lambda bi, qi: (bi, 0)),
                  pl.BlockSpec(wuv.shape, lambda bi, qi: (0, 0, 0))],
        out_specs=pl.BlockSpec((tq, H_A * V_A), lambda bi, qi: (bi * nq + qi, 0)),
        out_shape=jax.ShapeDtypeStruct((b * t, H_A * V_A), BF16),
        scratch_shapes=[pltpu.VMEM((t, KV_LORA), BF16),
                        pltpu.VMEM((t // tk, KV_LORA, tk), BF16),
                        pltpu.VMEM((1, cols), F32),
                        pltpu.VMEM((1, cols), F32),
                        pltpu.VMEM((KV_LORA, cols), F32)],
        compiler_params=_cparams("parallel", "arbitrary"),
        name="mla_prompt_attn",
    )(qlat, qpe, ckv, kpet, wuv)


def _page_copies(pt_ref, first, stride, g, caches, bufs, sems, slot):
    copies = []
    for i in range(g):
        page = pt_ref[first + stride * i]
        for k, (cache, buf) in enumerate(zip(caches, bufs)):
            copies.append(pltpu.make_async_copy(cache.at[page], buf.at[slot, i], sems.at[k, slot]))
    return copies


def _paged_step(pt_ref, g, caches, bufs, sems, first_of):
    ns = pl.num_programs(1)
    lin = pl.program_id(0) * ns + pl.program_id(1)
    total = pl.num_programs(0) * ns
    slot = lin % 2

    @pl.when(lin == 0)
    def _():
        for cp in _page_copies(pt_ref, *first_of(lin), g, caches, bufs, sems, slot):
            cp.start()

    @pl.when(lin + 1 < total)
    def _():
        for cp in _page_copies(pt_ref, *first_of(lin + 1), g, caches, bufs, sems, 1 - slot):
            cp.start()

    for cp in _page_copies(pt_ref, *first_of(lin), g, caches, bufs, sems, slot):
        cp.wait()
    return slot


def _mla_sample_body(pt_ref, qlat_ref, qpe_ref, ckn_ref, kpn_ref, wuv_ref, ck_hbm, kp_hbm,
                     o_ref, ql_ref, qp_ref, ckpad_ref, kppad_ref, m_ref, l_ref, acc_ref,
                     ckbuf, kpbuf, sems, *, tn, g, nchain):
    step = pl.program_id(1)
    rows = H_A * tn
    slot = _paged_step(pt_ref, g, (ck_hbm, kp_hbm), (ckbuf, kpbuf), sems, lambda n: (n * g, 1))

    def update(s, ck, carry):
        m_prev, l_prev, acc = carry
        m_new = jnp.maximum(m_prev, jnp.max(s, axis=-1, keepdims=True))
        alpha = jnp.exp(m_prev - m_new)
        p = jnp.exp(s - m_new)
        l_new = alpha * l_prev + jnp.sum(p, axis=-1, keepdims=True)
        return m_new, l_new, alpha * acc + _dot(p.astype(BF16), ck)

    @pl.when(step == 0)
    def _():
        ql_ref[...] = qlat_ref[...].reshape(rows, KV_LORA).astype(BF16)
        qps = _head_stack(qpe_ref[...], ROPE_A, H_A).astype(BF16)
        fr = lax.broadcasted_iota(jnp.int32, (ROPE_W, ROPE_A), 0)
        fc = lax.broadcasted_iota(jnp.int32, (ROPE_W, ROPE_A), 1)
        fold = jnp.where((fr & (ROPE_A - 1)) == fc, 1.0, 0.0).astype(BF16)
        qp_ref[...] = _dot(qps, fold).astype(BF16)
        ckpad_ref[...] = jnp.zeros(ckpad_ref.shape, F32)
        kppad_ref[...] = jnp.zeros(kppad_ref.shape, F32)
        ckpad_ref[0:tn, :] = ckn_ref[...]
        kppad_ref[0:tn, :] = kpn_ref[...]
        row = lax.broadcasted_iota(jnp.int32, (rows, PAGE), 0) & (tn - 1)
        col = lax.broadcasted_iota(jnp.int32, (rows, PAGE), 1)
        ck = ckpad_ref[...].astype(BF16)
        s = _dot_nt(ql_ref[...], ck) + _dot_nt(qp_ref[...], kppad_ref[...].astype(BF16))
        s = jnp.where(col <= row, s, -jnp.inf)
        init = (jnp.full((rows, 1), -jnp.inf, F32), jnp.zeros((rows, 1), F32),
                jnp.zeros((rows, KV_LORA), F32))
        m_ref[0], l_ref[0], acc_ref[0] = update(s, ck, init)
        for c in range(1, nchain):
            m_ref[c], l_ref[c], acc_ref[c] = init

    per = g // nchain
    for c in range(nchain):
        pages = range(c * per, (c + 1) * per)
        ck = jnp.concatenate([ckbuf[slot, i] for i in pages], axis=0).astype(BF16)
        kp = jnp.concatenate([kpbuf[slot, i] for i in pages], axis=1).astype(BF16)
        s = _dot_nt(ql_ref[...], ck) + _dot(qp_ref[...], kp)
        m_ref[c], l_ref[c], acc_ref[c] = update(s, ck, (m_ref[c], l_ref[c], acc_ref[c]))

    @pl.when(step == pl.num_programs(1) - 1)
    def _():
        m = m_ref[0]
        for c in range(1, nchain):
            m = jnp.maximum(m, m_ref[c])
        l = jnp.zeros((rows, 1), F32)
        acc = jnp.zeros((rows, KV_LORA), F32)
        for c in range(nchain):
            w = jnp.exp(m_ref[c] - m)
            l = l + w * l_ref[c]
            acc = acc + w * acc_ref[c]
        o_ref[...] = _uv_project(acc / l, wuv_ref, tn).astype(o_ref.dtype)


def _mla_sample_attn(page_table, qlat, qpe, ckn, kpn, wuv, cache_ckv, cache_kpe, tn, g):
    bs, n_pages = page_table.shape
    rows = H_A * tn
    pt = page_table.reshape(-1)
    nchain = 2 if g % 2 == 0 else 1
    in_specs = [pl.BlockSpec((H_A, tn, KV_LORA), lambda b, s, pt_ref: (0, b, 0)),
                pl.BlockSpec((tn, ROPE_W), lambda b, s, pt_ref: (b, 0)),
                pl.BlockSpec((tn, KV_LORA), lambda b, s, pt_ref: (b, 0)),
                pl.BlockSpec((tn, ROPE_A), lambda b, s, pt_ref: (b, 0)),
                pl.BlockSpec(wuv.shape, lambda b, s, pt_ref: (0, 0, 0)),
                pl.BlockSpec(memory_space=pl.ANY),
                pl.BlockSpec(memory_space=pl.ANY)]
    return pl.pallas_call(
        functools.partial(_mla_sample_body, tn=tn, g=g, nchain=nchain),
        grid_spec=pltpu.PrefetchScalarGridSpec(
            num_scalar_prefetch=1,
            grid=(bs, n_pages // g),
            in_specs=in_specs,
            out_specs=pl.BlockSpec((tn, H_A * V_A), lambda b, s, pt_ref: (b, 0)),
            scratch_shapes=[pltpu.VMEM((rows, KV_LORA), BF16),
                            pltpu.VMEM((rows, ROPE_A), BF16),
                            pltpu.VMEM((PAGE, KV_LORA), F32),
                            pltpu.VMEM((PAGE, ROPE_A), F32),
                            pltpu.VMEM((nchain, rows, 1), F32),
                            pltpu.VMEM((nchain, rows, 1), F32),
                            pltpu.VMEM((nchain, rows, KV_LORA), F32),
                            pltpu.VMEM((2, g, PAGE, KV_LORA), F32),
                            pltpu.VMEM((2, g, ROPE_A, PAGE), F32),
                            pltpu.SemaphoreType.DMA((2, 2))]),
        out_shape=jax.ShapeDtypeStruct((bs * tn, H_A * V_A), F32),
        compiler_params=_cparams("arbitrary", "arbitrary"),
        name="mla_sample_attn",
    )(pt, qlat, qpe, ckn, kpn, wuv, cache_ckv, cache_kpe)


def _sb_block(qs, k, v, rest, acc, upper, mask=None):
    z = _dot_nt(qs, k)
    l_neg = _log_sigmoid(-z)
    if mask is not None:
        l_neg = jnp.where(mask, l_neg, 0.0)
    hi, lo = _split_bf16(l_neg, 2)
    later = _dot(hi, upper) + _dot(lo, upper) + rest
    a = jnp.exp(z + l_neg + later)
    if mask is not None:
        a = jnp.where(mask, a, 0.0)
    acc = acc + _dot(a.astype(BF16), v)
    rest = rest + jnp.sum(l_neg, axis=-1, keepdims=True)
    return rest, acc


def _upper_ones(n):
    r = lax.broadcasted_iota(jnp.int32, (n, n), 0)
    c = lax.broadcasted_iota(jnp.int32, (n, n), 1)
    return jnp.where(r > c, 1.0, 0.0).astype(BF16)


def _head_collapse(acc, t, width, n_heads):
    lane = lax.broadcasted_iota(jnp.int32, (t, n_heads * width), 1)
    out = jnp.zeros((t, n_heads * width), F32)
    for h in range(n_heads):
        out = out + jnp.where((lane >= h * width) & (lane < (h + 1) * width), acc[h * t:(h + 1) * t], 0.0)
    return out


def _sb_prompt_body(q_ref, k_ref, v_ref, o_ref, kb_ref, vb_ref, rest_ref, acc_ref, *, tq, tk):
    qi = pl.program_id(1)
    rows = H_D * tq

    @pl.when(qi == 0)
    def _():
        kb_ref[...] = k_ref[...].astype(BF16)
        vb_ref[...] = v_ref[...].astype(BF16)

    qs = _head_stack(q_ref[...] * SB_SCALE, DH_D, H_D).astype(BF16)
    upper = _upper_ones(tk)
    rest_ref[...] = jnp.zeros(rest_ref.shape, F32)
    acc_ref[...] = jnp.zeros(acc_ref.shape, F32)

    def block(k0, masked):
        mask = None
        if masked:
            row = lax.broadcasted_iota(jnp.int32, (rows, tk), 0) & (tq - 1)
            col = lax.broadcasted_iota(jnp.int32, (rows, tk), 1)
            mask = col + k0 < row + qi * tq
        rest, acc = _sb_block(qs, kb_ref[pl.ds(k0, tk), :], vb_ref[pl.ds(k0, tk), :],
                              rest_ref[...], acc_ref[...], upper, mask)
        rest_ref[...] = rest
        acc_ref[...] = acc

    nfull = (qi * tq) // tk
    block(pl.multiple_of(nfull * tk, tk), True)

    def full_block(i, carry):
        block(pl.multiple_of((nfull - 1 - i) * tk, tk), False)
        return carry

    lax.fori_loop(0, nfull, full_block, 0)
    o_ref[...] = _head_collapse(acc_ref[...], tq, DH_D, H_D).astype(o_ref.dtype)


def _sb_prompt_attn(q, k, v, b, t):
    tq = min(256, t)
    tk = min(256, t)
    nq = t // tq
    rows = H_D * tq
    w = H_D * DH_D
    return pl.pallas_call(
        functools.partial(_sb_prompt_body, tq=tq, tk=tk),
        grid=(b, nq),
        in_specs=[pl.BlockSpec((tq, w), lambda bi, qi: (bi * nq + qi, 0)),
                  pl.BlockSpec((t, w), lambda bi, qi: (bi, 0)),
                  pl.BlockSpec((t, w), lambda bi, qi: (bi, 0))],
        out_specs=pl.BlockSpec((tq, w), lambda bi, qi: (bi * nq + qi, 0)),
        out_shape=jax.ShapeDtypeStruct((b * t, w), BF16),
        scratch_shapes=[pltpu.VMEM((t, w), BF16),
                        pltpu.VMEM((t, w), BF16),
                        pltpu.VMEM((rows, 1), F32),
                        pltpu.VMEM((rows, w), F32)],
        compiler_params=_cparams("parallel", "arbitrary"),
        name="sb_prompt_attn",
    )(q, k, v)


def _sb_sample_body(pt_ref, q_ref, kn_ref, vn_ref, k_hbm, v_hbm, o_ref, qs_ref, kpad_ref, vpad_ref,
                    rest_ref, acc_ref, kbuf, vbuf, sems, *, tn, g, ns):
    step = pl.program_id(1)
    n_pages = ns * g

    def newest_first(n):
        seq, s = n // ns, n % ns
        return seq * n_pages + n_pages - 1 - s * g, -1

    slot = _paged_step(pt_ref, g, (k_hbm, v_hbm), (kbuf, vbuf), sems, newest_first)
    rows = H_D * tn
    r = lax.broadcasted_iota(jnp.int32, (PAGE, 2 * PAGE), 0)
    c = lax.broadcasted_iota(jnp.int32, (PAGE, 2 * PAGE), 1)
    sums = jnp.where((c >= PAGE) | (r > c), 1.0, 0.0).astype(BF16)

    def page_sums(l_negs):
        split = [_split_bf16(x, 2) for x in l_negs]
        stacked = jnp.concatenate([s[0] for s in split] + [s[1] for s in split], axis=0)
        cs = _dot(stacked, sums)
        n = len(l_negs) * rows
        cs = cs[:n] + cs[n:]
        return [(cs[i * rows:(i + 1) * rows, :PAGE], cs[i * rows:(i + 1) * rows, PAGE:])
                for i in range(len(l_negs))]

    @pl.when(step == 0)
    def _():
        qs_ref[...] = _head_stack(q_ref[...] * SB_SCALE, DH_D, H_D).astype(BF16)
        kpad_ref[...] = jnp.zeros(kpad_ref.shape, F32)
        vpad_ref[...] = jnp.zeros(vpad_ref.shape, F32)
        kpad_ref[0:tn, :] = kn_ref[...]
        vpad_ref[0:tn, :] = vn_ref[...]
        row = lax.broadcasted_iota(jnp.int32, (rows, PAGE), 0) & (tn - 1)
        col = lax.broadcasted_iota(jnp.int32, (rows, PAGE), 1)
        mask = col < row
        z = _dot_nt(qs_ref[...], kpad_ref[...].astype(BF16))
        l_neg = jnp.where(mask, _log_sigmoid(-z), 0.0)
        (later, total), = page_sums([l_neg])
        a = jnp.where(mask, jnp.exp(z + l_neg + later), 0.0)
        acc_ref[...] = _dot(a.astype(BF16), vpad_ref[...].astype(BF16))
        rest_ref[...] = total

    kt = jnp.concatenate([kbuf[slot, i] for i in range(g)], axis=1).astype(BF16)
    vt = jnp.concatenate([vbuf[slot, i] for i in range(g)], axis=1).astype(BF16)
    z = _dot(qs_ref[...], kt)
    l_neg = _log_sigmoid(-z)
    zl = z + l_neg
    stick = rest_ref[...]
    logits = []
    for i, (later, total) in enumerate(page_sums([l_neg[:, i * PAGE:(i + 1) * PAGE] for i in range(g)])):
        logits.append(zl[:, i * PAGE:(i + 1) * PAGE] + later + stick)
        stick = stick + total
    a = jnp.exp(jnp.concatenate(logits, axis=1)).astype(BF16)
    acc_ref[...] += _dot_nt(a, vt)
    rest_ref[...] = stick

    @pl.when(step == pl.num_programs(1) - 1)
    def _():
        o_ref[...] = _head_collapse(acc_ref[...], tn, DH_D, H_D).astype(o_ref.dtype)


def _sb_sample_attn(page_table, q, kn, vn, cache_k, cache_v, tn, g):
    bs, n_pages = page_table.shape
    rows = H_D * tn
    w = H_D * DH_D
    pt = page_table.reshape(-1)

    in_specs = [pl.BlockSpec((tn, w), lambda b, s, pt_ref: (b, 0))] * 3
    in_specs += [pl.BlockSpec(memory_space=pl.ANY)] * 2
    return pl.pallas_call(
        functools.partial(_sb_sample_body, tn=tn, g=g, ns=n_pages // g),
        grid_spec=pltpu.PrefetchScalarGridSpec(
            num_scalar_prefetch=1,
            grid=(bs, n_pages // g),
            in_specs=in_specs,
            out_specs=pl.BlockSpec((tn, w), lambda b, s, pt_ref: (b, 0)),
            scratch_shapes=[pltpu.VMEM((rows, w), BF16),
                            pltpu.VMEM((PAGE, w), F32),
                            pltpu.VMEM((PAGE, w), F32),
                            pltpu.VMEM((rows, PAGE), F32),
                            pltpu.VMEM((rows, w), F32),
                            pltpu.VMEM((2, g, w, PAGE), F32),
                            pltpu.VMEM((2, g, w, PAGE), F32),
                            pltpu.SemaphoreType.DMA((2, 2))]),
        out_shape=jax.ShapeDtypeStruct((bs * tn, w), F32),
        compiler_params=_cparams("arbitrary", "arbitrary"),
        name="sb_sample_attn",
    )(pt, q, kn, vn, cache_k, cache_v)


def _gated_body(*refs, mode, lq, nblk, ngrp, dk, dv, dkp, dvp, has_s0):
    nh = 4
    nsq = CHUNK // lq
    it = iter(refs)
    if mode == "hgrn":
        zq_ref, zf_ref, zv_ref, zg_ref, lb_ref, nw_ref = (next(it) for _ in range(6))
    else:
        zq_ref, zk_ref, zv_ref, zg_ref, ga_ref, wg_hi_ref, wg_lo_ref, bg_ref, nw_ref = (next(it) for _ in range(9))
    s0_ref = next(it) if has_s0 else None
    o_ref, sout_ref, s_ref = next(it), next(it), next(it)
    step = pl.program_id(1)

    if has_s0:
        s_ref[...] = jnp.zeros(s_ref.shape, F32)
        s_ref[:, :, 0:dk, 0:dv] = s0_ref[...]
    else:
        @pl.when(step == 0)
        def _():
            s_ref[...] = jnp.zeros(s_ref.shape, F32)

    r = lax.broadcasted_iota(jnp.int32, (CHUNK, CHUNK), 0)
    c = lax.broadcasted_iota(jnp.int32, (CHUNK, CHUNK), 1)
    seq_r, seq_c = r // lq, c // lq
    same = seq_r == seq_c
    causal = same & (r >= c)
    ref_pos = seq_r * lq + lq // 2
    ones = lambda m: jnp.where(m, 1.0, 0.0).astype(BF16)
    sums = jnp.concatenate([ones(causal), ones(same & (c <= ref_pos)), ones(same)], axis=0)
    row_seq = lax.broadcasted_iota(jnp.int32, (CHUNK, dkp), 0) // lq

    def chunk(gi, ci):
        rows = pl.ds(pl.multiple_of(ci * CHUNK, CHUNK), CHUNK)
        if mode == "hgrn":
            lb = lb_ref[...]
            f = lb + (1.0 - lb) * _sigmoid(zf_ref[gi, rows, :])
            logf, k = jnp.log(f), 1.0 - f
            zq = zq_ref[gi, rows, :]
            q = zq * _sigmoid(zq)
        else:
            ga_hi, ga_lo = _split_bf16(ga_ref[gi, rows, :], 2)
            logits = (_dot(ga_hi, wg_hi_ref[...]) + _dot(ga_hi, wg_lo_ref[...])
                      + _dot(ga_lo, wg_hi_ref[...]) + bg_ref[...])
            logf = _log_sigmoid(logits) * (1.0 / GATE_NORM)
            k = zk_ref[gi, rows, :]
            q = zq_ref[gi, rows, :] * (dk ** -0.5)
        v = zv_ref[gi, rows, :].astype(BF16)
        parts = _split_bf16(logf, 3)
        gall = _dot(sums, parts[0]) + _dot(sums, parts[1]) + _dot(sums, parts[2])
        g, gr, gl = gall[:CHUNK], gall[CHUNK:2 * CHUNK], gall[2 * CHUNK:]
        qg = (q * jnp.exp(g - gr)).astype(BF16)
        kg = (k * jnp.exp(gr - g)).astype(BF16)
        qs = q * jnp.exp(g)
        kd = k * jnp.exp(gl - g)
        el = jnp.exp(gl)
        gate = zg_ref[gi, rows, :]
        outs = []
        for h in range(nh):
            ks, vs = slice(h * dkp, (h + 1) * dkp), slice(h * dvp, (h + 1) * dvp)
            a = jnp.where(causal, _dot_nt(qg[:, ks], kg[:, ks]), 0.0)
            o = _dot(a.astype(BF16), v[:, vs])
            for s in range(nsq):
                if nsq == 1:
                    qs_s, kd_s = qs[:, ks], kd[:, ks]
                else:
                    qs_s = jnp.where(row_seq == s, qs[:, ks], 0.0)
                    kd_s = jnp.where(row_seq == s, kd[:, ks], 0.0)
                st = s_ref[gi * nsq + s, h]
                o = o + _dot(qs_s.astype(BF16), st.astype(BF16))
                e_col = jnp.broadcast_to(el[s * lq:s * lq + 1, ks], (dkp, dkp)).T
                if dvp > dkp:
                    e_col = jnp.concatenate([e_col] * (dvp // dkp), axis=1)
                s_ref[gi * nsq + s, h] = e_col * st + _dot_tn(kd_s.astype(BF16), v[:, vs])
            nwh = nw_ref[...]
            gh = gate[:, vs]
            outs.append(_rms(o, nwh, dv) * (gh * _sigmoid(gh)))
        o_ref[gi, rows, :] = jnp.concatenate(outs, axis=-1).astype(o_ref.dtype)

    if nblk == 1:
        for gi in range(ngrp):
            chunk(gi, 0)
    else:
        def loop_body(ci, carry):
            for gi in range(ngrp):
                chunk(gi, ci)
            return carry
        lax.fori_loop(0, nblk, loop_body, 0)

    if has_s0:
        sout_ref[...] = s_ref[:, :, 0:dk, 0:dv]
    else:
        @pl.when(step == pl.num_programs(1) - 1)
        def _():
            sout_ref[...] = s_ref[:, :, 0:dk, 0:dv]


def _gated_linear(mode, arrays, col_blocks, params, s0, b, t, dk, dv, dkp, dvp):
    nh = 4
    m = b * t
    lq = min(CHUNK, t)
    nsq = CHUNK // lq
    if s0 is None:
        n_groups, group_rows = b, t
    else:
        n_groups, group_rows = m // CHUNK, CHUNK
    ngrp = max(n for n in ((4, 2, 1) if s0 is None else (2, 1)) if n_groups % n == 0)
    nblk = min(16 // ngrp, group_rows // CHUNK)
    tr = CHUNK * nblk
    grid = (n_groups // ngrp, group_rows // tr)
    state_map = lambda bi, si: (bi, 0, 0, 0)
    arrays = [a.reshape(n_groups, group_rows, a.shape[-1]) for a in arrays]
    in_specs = [pl.BlockSpec((ngrp, tr, w), functools.partial(lambda bi, si, cb: (bi, si, cb), cb=cb))
                for (w, cb) in col_blocks]
    in_specs += [pl.BlockSpec(p.shape, lambda bi, si: (0, 0)) for p in params]
    operands = list(arrays) + list(params)
    if s0 is not None:
        in_specs.append(pl.BlockSpec((ngrp * nsq, nh, dk, dv), state_map))
        operands.append(s0)
    o, s_out = pl.pallas_call(
        functools.partial(_gated_body, mode=mode, lq=lq, nblk=nblk, ngrp=ngrp, dk=dk, dv=dv, dkp=dkp,
                          dvp=dvp, has_s0=s0 is not None),
        grid=grid,
        in_specs=in_specs,
        out_specs=[pl.BlockSpec((ngrp, tr, nh * dvp), lambda bi, si: (bi, si, 0)),
                   pl.BlockSpec((ngrp * nsq, nh, dk, dv), state_map)],
        out_shape=[jax.ShapeDtypeStruct((n_groups, group_rows, nh * dvp), BF16),
                   jax.ShapeDtypeStruct((b, nh, dk, dv), F32)],
        scratch_shapes=[pltpu.VMEM((ngrp * nsq, nh, dkp, dvp), F32)],
        compiler_params=_cparams("parallel", "arbitrary"),
        name="gated_linear_" + mode,
    )(*operands)
    return o.reshape(m, nh * dvp), s_out


def _ffn_body(x_ref, oa_ref, ob_ref, woa_ref, wob_ref, fnw_ref, wg_ref, wu_ref, wo_ref, finw_ref,
              y_ref, h_ref, xn_ref, acc_ref, *, final_norm):
    j = pl.program_id(1)

    @pl.when(j == 0)
    def _():
        h = (x_ref[...] + _dot(oa_ref[...].astype(BF16), woa_ref[...])
             + _dot(ob_ref[...].astype(BF16), wob_ref[...]))
        h_ref[...] = h
        xn_ref[...] = _rms(h, fnw_ref[...]).astype(BF16)
        acc_ref[...] = jnp.zeros(acc_ref.shape, F32)

    xn = xn_ref[...]
    gate = _dot(xn, wg_ref[...])
    up = _dot(xn, wu_ref[...])
    act = (gate * _sigmoid(gate) * up).astype(BF16)
    acc_ref[...] += _dot(act, wo_ref[...])

    @pl.when(j == pl.num_programs(1) - 1)
    def _():
        y = h_ref[...] + acc_ref[...]
        if final_norm:
            y = _rms(y, finw_ref[...])
        y_ref[...] = y


def _mix_ffn(x, oa, ob, woa, wob, fnw, w_in, w_out, finw, final_norm):
    m = x.shape[0]
    tm = min(512, m)
    tf = D_FF // 2
    nf = D_FF // tf
    ka, kb = oa.shape[1], ob.shape[1]
    return pl.pallas_call(
        functools.partial(_ffn_body, final_norm=final_norm),
        grid=(m // tm, nf),
        in_specs=[pl.BlockSpec((tm, D_MODEL), lambda i, j: (i, 0)),
                  pl.BlockSpec((tm, ka), lambda i, j: (i, 0)),
                  pl.BlockSpec((tm, kb), lambda i, j: (i, 0)),
                  pl.BlockSpec((ka, D_MODEL), lambda i, j: (0, 0)),
                  pl.BlockSpec((kb, D_MODEL), lambda i, j: (0, 0)),
                  pl.BlockSpec((1, D_MODEL), lambda i, j: (0, 0)),
                  pl.BlockSpec((D_MODEL, tf), lambda i, j: (0, j)),
                  pl.BlockSpec((D_MODEL, tf), lambda i, j: (0, j + nf)),
                  pl.BlockSpec((tf, D_MODEL), lambda i, j: (j, 0)),
                  pl.BlockSpec((1, D_MODEL), lambda i, j: (0, 0))],
        out_specs=pl.BlockSpec((tm, D_MODEL), lambda i, j: (i, 0)),
        out_shape=jax.ShapeDtypeStruct((m, D_MODEL), F32),
        scratch_shapes=[pltpu.VMEM((tm, D_MODEL), F32),
                        pltpu.VMEM((tm, D_MODEL), BF16),
                        pltpu.VMEM((tm, D_MODEL), F32)],
        compiler_params=_cparams("parallel", "arbitrary"),
        name="mix_ffn",
    )(x, oa, ob, woa, wob, fnw.reshape(1, -1), w_in, w_in, w_out, finw.reshape(1, -1))


def _pad_heads(w, nh, d, dp, axis):
    shape = w.shape
    w = w.reshape(shape[:axis] + (nh, d) + shape[axis + 1:])
    pad = [(0, 0)] * w.ndim
    pad[axis + 1] = (0, dp - d)
    w = jnp.pad(w, pad)
    return w.reshape(shape[:axis] + (nh * dp,) + shape[axis + 1:])


def _rope_tables(pos):
    inv = ROPE_THETA ** (-jnp.arange(0, ROPE_A, 2, dtype=F32) / ROPE_A)
    ang = pos[:, None] * inv[None, :]
    cos = jnp.repeat(jnp.cos(ang), 2, axis=1)
    sin = jnp.repeat(jnp.sin(ang), 2, axis=1) * jnp.tile(jnp.array([-1.0, 1.0], F32), ROPE_A // 2)[None, :]
    return jnp.tile(cos, (1, H_A)), jnp.tile(sin, (1, H_A))


def _prepare_weights(P):
    W = {}
    w_ab = P["w_in_ab"][0]
    o_ckv, o_kpe, o_h = Q_LORA, Q_LORA + KV_LORA, Q_LORA + KV_LORA + ROPE_A
    W["w_mla"] = jnp.concatenate(
        [w_ab[:, :o_kpe], jnp.tile(w_ab[:, o_kpe:o_h], (1, H_A))], axis=1).astype(BF16)
    W["w_hgrn"] = w_ab[:, o_h:].astype(BF16)
    wuq = P["mla_w_uq"][0].reshape(Q_LORA, H_A, NOPE_A + ROPE_A)
    W["w_uq"] = jnp.concatenate([wuq[:, :, :NOPE_A].reshape(Q_LORA, -1),
                                 wuq[:, :, NOPE_A:].reshape(Q_LORA, -1)], axis=1).astype(BF16)
    half = (jnp.arange(H_A) % 2)[:, None, None]
    wk = jnp.transpose(P["mla_w_uk"][0], (1, 2, 0))
    W["w_uk"] = jnp.where(half == 0, jnp.pad(wk, ((0, 0), (0, NOPE_A), (0, 0))),
                          jnp.pad(wk, ((0, 0), (NOPE_A, 0), (0, 0)))).astype(BF16)
    W["w_uk_t"] = jnp.transpose(W["w_uk"], (0, 2, 1))
    wv = jnp.transpose(P["mla_w_uv"][0], (1, 0, 2))
    W["w_uv"] = jnp.where(half == 0, jnp.pad(wv, ((0, 0), (0, 0), (0, V_A))),
                          jnp.pad(wv, ((0, 0), (0, 0), (V_A, 0)))).astype(BF16)
    lb = jnp.cumsum(jax.nn.softmax(P["hgrn_lb_logits"].astype(F32), axis=0), axis=0)[0]
    W["hgrn_lb"] = lb.reshape(1, -1)
    W["hgrn_nw"] = P["hgrn_norm_w"][0].reshape(1, -1)
    w_out_ab = P["w_out_ab"][0].astype(BF16)
    W["w_out_a"], W["w_out_b"] = w_out_ab[:H_A * V_A], w_out_ab[H_A * V_A:]

    w_cd = P["w_in_cd"][0]
    offs = [0]
    for wdt in (H_C * DK_C, H_C * DK_C, H_C * DV_C, H_C * DV_C, GATE_RANK, H_D * DH_D * 3):
        offs.append(offs[-1] + wdt)
    seg = lambda i: w_cd[:, offs[i]:offs[i + 1]]
    W["w_gla"] = jnp.concatenate(
        [_pad_heads(seg(0), H_C, DK_C, DKP_C, 1), _pad_heads(seg(1), H_C, DK_C, DKP_C, 1),
         _pad_heads(seg(2), H_C, DV_C, DVP_C, 1), _pad_heads(seg(3), H_C, DV_C, DVP_C, 1),
         jnp.pad(seg(4), ((0, 0), (0, GATE_RANK_P - GATE_RANK)))], axis=1).astype(BF16)
    W["w_sb"] = seg(5).astype(BF16)
    wg2 = jnp.pad(_pad_heads(P["gla_w_gate2"][0], H_C, DK_C, DKP_C, 1), ((0, GATE_RANK_P - GATE_RANK), (0, 0)))
    W["wg2_hi"] = wg2.astype(BF16)
    W["wg2_lo"] = (wg2 - W["wg2_hi"].astype(F32)).astype(BF16)
    W["bg2"] = _pad_heads(P["gla_b_gate2"][0].reshape(1, -1), H_C, DK_C, DKP_C, 1)
    W["gla_nw"] = jnp.pad(P["gla_norm_w"][0], (0, DVP_C - DV_C)).reshape(1, -1)
    w_out_cd = P["w_out_cd"][0]
    W["w_out_c"] = _pad_heads(w_out_cd[:H_C * DV_C], H_C, DV_C, DVP_C, 0).astype(BF16)
    W["w_out_d"] = w_out_cd[H_C * DV_C:].astype(BF16)
    W["w_ffn_in"] = [P["w_ffn_in"][l].astype(BF16) for l in range(2)]
    W["w_ffn_out"] = [P["w_ffn_out"][l].astype(BF16) for l in range(2)]
    return W


def _trunk(x3, pos, P, W, ctx):
    b, t, _ = x3.shape
    m = b * t
    x = x3.reshape(m, D_MODEL)
    sample = ctx is not None
    cos, sin = _rope_tables(pos)
    reps = min(512, m) // t
    if reps > 1:
        cos, sin = jnp.tile(cos, (reps, 1)), jnp.tile(sin, (reps, 1))

    qlat, qpe, ckv, kpe, kpet = _mla_prep(
        x, P["attn_norm_w"][0], W["w_mla"], P["mla_q_norm_w"][0], W["w_uq"],
        W["w_uk"] if sample else W["w_uk_t"], P["mla_kv_norm_w"][0], cos, sin,
        F32 if sample else BF16, transposed_q=not sample)
    zh = _norm_matmul(x, P["attn_norm_w"][0], W["w_hgrn"], H_B * DK_B, split=True)
    if sample:
        o_a = _mla_sample_attn(ctx["page_table"], qlat, qpe, ckv, kpe, W["w_uv"],
                               ctx["cache_mla_ckv"], ctx["cache_mla_kpe"], t, ctx["pages_per_step"])
        s0 = ctx["state_hgrn"][0]
    else:
        o_a = _mla_prompt_attn(qlat, qpe, ckv, kpet, W["w_uv"], b, t)
        s0 = None
    wb = H_B * DK_B
    o_b, s_hgrn = _gated_linear("hgrn", zh, [(wb, 0)] * 4, [W["hgrn_lb"], W["hgrn_nw"]], s0,
                                b, t, DK_B, DV_B, DK_B, DV_B)
    h1 = _mix_ffn(x, o_a, o_b, W["w_out_a"], W["w_out_b"], P["ffn_norm_w"][0],
                  W["w_ffn_in"][0], W["w_ffn_out"][0], P["final_norm_w"], final_norm=False)

    zg = _norm_matmul(h1, P["attn_norm_w"][1], W["w_gla"], 640, split=False)
    sq, sk, sv = _norm_matmul(h1, P["attn_norm_w"][1], W["w_sb"], H_D * DH_D, split=True)
    if sample:
        o_d = _sb_sample_attn(ctx["page_table"], sq, sk, sv, ctx["cache_sb_k"], ctx["cache_sb_v"],
                              t, ctx["pages_per_step"])
        s0 = ctx["state_gla"][0]
    else:
        o_d = _sb_prompt_attn(sq, sk, sv, b, t)
        s0 = None
    wq, wv = H_C * DKP_C, H_C * DVP_C
    o_c, s_gla = _gated_linear(
        "gla", [zg] * 5, [(wq, 0), (wq, 1), (wv, 1), (wv, 2), (GATE_RANK_P, (2 * wq + 2 * wv) // GATE_RANK_P)],
        [W["wg2_hi"], W["wg2_lo"], W["bg2"], W["gla_nw"]], s0, b, t, DK_C, DV_C, DKP_C, DVP_C)
    y = _mix_ffn(h1, o_c, o_d, W["w_out_c"], W["w_out_d"], P["ffn_norm_w"][1],
                 W["w_ffn_in"][1], W["w_ffn_out"][1], P["final_norm_w"], final_norm=True)

    new = (ckv.reshape(1, b, t, KV_LORA), kpe.reshape(1, b, t, ROPE_A), s_hgrn[None],
           sk.reshape(1, b, t, H_D, DH_D), sv.reshape(1, b, t, H_D, DH_D), s_gla[None])
    return y.reshape(b, t, D_MODEL), new


def kernel(x_prompt, x_sample, cache_mla_ckv, cache_mla_kpe, state_hgrn, cache_sb_k, cache_sb_v, state_gla, page_table, attn_norm_w, ffn_norm_w, final_norm_w, w_in_ab, mla_q_norm_w, mla_w_uq, mla_kv_norm_w, mla_w_uk, mla_w_uv, hgrn_lb_logits, hgrn_norm_w, w_out_ab, w_in_cd, gla_w_gate2, gla_b_gate2, gla_norm_w, w_out_cd, w_ffn_in, w_ffn_out):
    P = {
        "attn_norm_w": attn_norm_w, "ffn_norm_w": ffn_norm_w, "final_norm_w": final_norm_w,
        "w_in_ab": w_in_ab, "mla_q_norm_w": mla_q_norm_w, "mla_w_uq": mla_w_uq,
        "mla_kv_norm_w": mla_kv_norm_w, "mla_w_uk": mla_w_uk, "mla_w_uv": mla_w_uv,
        "hgrn_lb_logits": hgrn_lb_logits, "hgrn_norm_w": hgrn_norm_w, "w_out_ab": w_out_ab,
        "w_in_cd": w_in_cd, "gla_w_gate2": gla_w_gate2, "gla_b_gate2": gla_b_gate2,
        "gla_norm_w": gla_norm_w, "w_out_cd": w_out_cd, "w_ffn_in": w_ffn_in, "w_ffn_out": w_ffn_out,
    }
    W = _prepare_weights(P)
    n_pages = page_table.shape[1]
    past = n_pages * PAGE
    pos_prompt = jnp.arange(x_prompt.shape[1], dtype=F32)
    pos_sample = jnp.arange(x_sample.shape[1], dtype=F32) + float(past)
    n_pool = cache_sb_k.shape[1]
    ctx = {
        "page_table": page_table,
        "cache_mla_ckv": cache_mla_ckv[0],
        "cache_mla_kpe": jnp.transpose(cache_mla_kpe[0], (0, 2, 1)),
        "state_hgrn": state_hgrn,
        "cache_sb_k": jnp.transpose(cache_sb_k[0], (0, 2, 3, 1)).reshape(n_pool, H_D * DH_D, PAGE),
        "cache_sb_v": jnp.transpose(cache_sb_v[0], (0, 2, 3, 1)).reshape(n_pool, H_D * DH_D, PAGE),
        "state_gla": state_gla,
        "pages_per_step": min(32, n_pages),
    }
    y_p, new_p = _trunk(x_prompt, pos_prompt, P, W, None)
    y_s, new_s = _trunk(x_sample, pos_sample, P, W, ctx)
    return (y_p, y_s) + new_p + new_s
```

```python
import functools

import jax
import jax.numpy as jnp
from jax import lax
from jax.experimental import pallas as pl
from jax.experimental.pallas import tpu as pltpu

F32 = jnp.float32
BF16 = jnp.bfloat16

D_MODEL = 1024
PAGE = 128
H_A, NOPE_A, ROPE_A, V_A = 8, 64, 32, 64
Q_LORA, KV_LORA = 384, 256
ROPE_THETA = 10000.0
MLA_SCALE = (NOPE_A + ROPE_A) ** -0.5
H_B, DK_B, DV_B = 4, 128, 128
H_C, DK_C, DV_C = 4, 96, 192
DKP_C, DVP_C = 128, 256
GATE_RANK, GATE_RANK_P = 16, 128
GATE_NORM = 16.0
H_D, DH_D = 4, 64
SB_SCALE = DH_D ** -0.5
D_FF = 2816
EPS = 1e-6
CHUNK = 64
ROPE_W = H_A * ROPE_A

VMEM_LIMIT = 56 * 1024 * 1024


def _cparams(*sem):
    return pltpu.CompilerParams(dimension_semantics=sem, vmem_limit_bytes=VMEM_LIMIT)


def _dot(a, b):
    return jnp.dot(a, b, preferred_element_type=F32)


def _dot_nt(a, b):
    return lax.dot_general(a, b, (((1,), (1,)), ((), ())), preferred_element_type=F32)


def _dot_tn(a, b):
    return lax.dot_general(a, b, (((0,), (0,)), ((), ())), preferred_element_type=F32)


def _split_bf16(x, n):
    parts = []
    for _ in range(n - 1):
        p = x.astype(BF16)
        parts.append(p)
        x = x - p.astype(F32)
    parts.append(x.astype(BF16))
    return parts


def _rms(x, w, n=None):
    n = x.shape[-1] if n is None else n
    ms = jnp.sum(x * x, axis=-1, keepdims=True) * (1.0 / n)
    return x * lax.rsqrt(ms + EPS) * w


def _sigmoid(x):
    return 1.0 / (1.0 + jnp.exp(-x))


def _log_sigmoid(x):
    return jnp.minimum(x, 0.0) - jnp.log(1.0 + jnp.exp(-jnp.abs(x)))


def _rope(x, cos, sin_signed):
    n = x.shape[-1]
    lane = lax.broadcasted_iota(jnp.int32, x.shape, x.ndim - 1)
    nxt = pltpu.roll(x, n - 1, x.ndim - 1)
    prv = pltpu.roll(x, 1, x.ndim - 1)
    swap = jnp.where((lane & 1) == 0, nxt, prv)
    return x * cos + swap * sin_signed


def _norm_matmul_body(x_ref, nw_ref, w_ref, *rest, starts):
    n_out = len(starts) - 1
    outs, xn_ref = rest[:n_out], rest[n_out]
    j = pl.program_id(1)

    @pl.when(j == 0)
    def _():
        xn_ref[...] = _rms(x_ref[...], nw_ref[...]).astype(BF16)

    z = _dot(xn_ref[...], w_ref[...])
    for g in range(n_out):
        @pl.when((j >= starts[g]) & (j < starts[g + 1]))
        def _(g=g):
            outs[g][...] = z.astype(outs[g].dtype)


def _norm_matmul(x, nw, w, tn, groups):
    m, kdim = x.shape
    tm = min(1024, m)
    starts = [0]
    for n_tiles, _ in groups:
        starts.append(starts[-1] + n_tiles)
    nj = starts[-1]
    assert nj * tn == w.shape[1]
    out_shape = [jax.ShapeDtypeStruct((m, n_tiles * tn), dt) for n_tiles, dt in groups]
    out_specs = [pl.BlockSpec((tm, tn), functools.partial(
        lambda i, j, s, n: (i, jnp.clip(j - s, 0, n - 1)), s=starts[g], n=groups[g][0]))
        for g in range(len(groups))]
    outs = pl.pallas_call(
        functools.partial(_norm_matmul_body, starts=tuple(starts)),
        grid=(m // tm, nj),
        in_specs=[pl.BlockSpec((tm, kdim), lambda i, j: (i, 0)),
                  pl.BlockSpec((1, kdim), lambda i, j: (0, 0)),
                  pl.BlockSpec((kdim, tn), lambda i, j: (0, j))],
        out_specs=out_specs,
        out_shape=out_shape,
        scratch_shapes=[pltpu.VMEM((tm, kdim), BF16)],
        compiler_params=_cparams("parallel", "arbitrary"),
        name="norm_matmul",
    )(x, nw.reshape(1, kdim), w)
    return outs


def _mla_prep_body(x_ref, anw_ref, w_ref, qnw_ref, wuq_ref, wk_ref, kvnw_ref, cos_ref, sin_ref,
                   qlat_ref, qpe_ref, ckv_ref, kpe_ref, kpet_ref, *, transposed_q):
    xn = _rms(x_ref[...], anw_ref[...]).astype(BF16)
    z = _dot(xn, w_ref[...])
    cqn = _rms(z[:, :Q_LORA], qnw_ref[...]).astype(BF16)
    q = _dot(cqn, wuq_ref[...])
    qn = q[:, :H_A * NOPE_A].astype(BF16)
    for h in range(H_A):
        qpair = qn[:, 128 * (h // 2):128 * (h // 2 + 1)]
        ql = _dot_nt(wk_ref[h], qpair) if transposed_q else _dot(qpair, wk_ref[h])
        qlat_ref[h] = (ql * MLA_SCALE).astype(qlat_ref.dtype)
    cos, sin = cos_ref[...], sin_ref[...]
    qr = _rope(q[:, H_A * NOPE_A:], cos, sin) * MLA_SCALE
    qpe_ref[...] = (qr.T if transposed_q else qr).astype(qpe_ref.dtype)
    ckv_ref[...] = _rms(z[:, Q_LORA:Q_LORA + KV_LORA], kvnw_ref[...])
    kr = _rope(z[:, Q_LORA + KV_LORA:], cos, sin)
    kpet_ref[...] = kr.astype(BF16)
    kpe_ref[...] = kr[:, :ROPE_A]


def _mla_prep(x, anw, w_mla, qnw, wuq, wk, kvnw, cos, sin, q_dtype, transposed_q):
    m = x.shape[0]
    tm = min(512, m)
    nt = cos.shape[0] // tm
    wz = w_mla.shape[1]
    if transposed_q:
        q_specs = [pl.BlockSpec((H_A, KV_LORA, tm), lambda i: (0, 0, i)),
                   pl.BlockSpec((ROPE_W, tm), lambda i: (0, i))]
        q_shapes = [jax.ShapeDtypeStruct((H_A, KV_LORA, m), q_dtype),
                    jax.ShapeDtypeStruct((ROPE_W, m), q_dtype)]
    else:
        q_specs = [pl.BlockSpec((H_A, tm, KV_LORA), lambda i: (0, i, 0)),
                   pl.BlockSpec((tm, ROPE_W), lambda i: (i, 0))]
        q_shapes = [jax.ShapeDtypeStruct((H_A, m, KV_LORA), q_dtype),
                    jax.ShapeDtypeStruct((m, ROPE_W), q_dtype)]
    return pl.pallas_call(
        functools.partial(_mla_prep_body, transposed_q=transposed_q),
        grid=(m // tm,),
        in_specs=[pl.BlockSpec((tm, D_MODEL), lambda i: (i, 0)),
                  pl.BlockSpec((1, D_MODEL), lambda i: (0, 0)),
                  pl.BlockSpec((D_MODEL, wz), lambda i: (0, 0)),
                  pl.BlockSpec((1, Q_LORA), lambda i: (0, 0)),
                  pl.BlockSpec(wuq.shape, lambda i: (0, 0)),
                  pl.BlockSpec(wk.shape, lambda i: (0, 0, 0)),
                  pl.BlockSpec((1, KV_LORA), lambda i: (0, 0)),
                  pl.BlockSpec((tm, ROPE_W), lambda i: (i % nt, 0)),
                  pl.BlockSpec((tm, ROPE_W), lambda i: (i % nt, 0))],
        out_specs=q_specs + [
                   pl.BlockSpec((tm, KV_LORA), lambda i: (i, 0)),
                   pl.BlockSpec((tm, ROPE_A), lambda i: (i, 0)),
                   pl.BlockSpec((tm, ROPE_W), lambda i: (i, 0))],
        out_shape=q_shapes + [
                   jax.ShapeDtypeStruct((m, KV_LORA), F32),
                   jax.ShapeDtypeStruct((m, ROPE_A), F32),
                   jax.ShapeDtypeStruct((m, ROPE_W), BF16)],
        compiler_params=_cparams("parallel"),
        name="mla_prep",
    )(x, anw.reshape(1, -1), w_mla, qnw.reshape(1, -1), wuq, wk, kvnw.reshape(1, -1), cos, sin)


def _head_stack(x, width, n_heads):
    lane = lax.broadcasted_iota(jnp.int32, x.shape, 1)
    zero = jnp.zeros_like(x)
    return jnp.concatenate(
        [jnp.where((lane >= h * width) & (lane < (h + 1) * width), x, zero) for h in range(n_heads)], axis=0)


def _uv_project(o, wuv_ref, t):
    ob = o.astype(BF16)
    pieces = []
    for p in range(H_A // 2):
        pieces.append(_dot(ob[2 * p * t:(2 * p + 1) * t], wuv_ref[2 * p])
                      + _dot(ob[(2 * p + 1) * t:(2 * p + 2) * t], wuv_ref[2 * p + 1]))
    return jnp.concatenate(pieces, axis=-1)


def _mla_prompt_body(qlat_ref, qpe_ref, ckv_ref, kpet_ref, wuv_ref, o_ref,
                     ckb_ref, ckt_ref, m_ref, l_ref, acc_ref, *, tq, tk):
    qi = pl.program_id(1)
    nkb = ckt_ref.shape[0]

    @pl.when(qi == 0)
    def _():
        ckb_ref[...] = ckv_ref[...].astype(BF16)
        for kb in range(nkb):
            ckt_ref[kb] = ckv_ref[kb * tk:(kb + 1) * tk, :].T.astype(BF16)

    ql = jnp.concatenate([qlat_ref[h] for h in range(H_A)], axis=1)
    qp = qpe_ref[...]
    feat = lax.broadcasted_iota(jnp.int32, qp.shape, 0)
    zero = jnp.zeros_like(qp)
    qp = jnp.concatenate([jnp.where((feat >= h * ROPE_A) & (feat < (h + 1) * ROPE_A), qp, zero)
                          for h in range(H_A)], axis=1)
    m_ref[...] = jnp.full(m_ref.shape, -jnp.inf, F32)
    l_ref[...] = jnp.zeros(l_ref.shape, F32)
    acc_ref[...] = jnp.zeros(acc_ref.shape, F32)

    def block(kb, masked):
        k0 = pl.multiple_of(kb * tk, tk)
        s = _dot(ckb_ref[pl.ds(k0, tk), :], ql) + _dot(kpet_ref[pl.ds(k0, tk), :], qp)
        if masked:
            key = lax.broadcasted_iota(jnp.int32, s.shape, 0) + k0
            tok = (lax.broadcasted_iota(jnp.int32, s.shape, 1) & (tq - 1)) + qi * tq
            s = jnp.where(key <= tok, s, -jnp.inf)
        m_prev = m_ref[...]
        m_new = jnp.maximum(m_prev, jnp.max(s, axis=0, keepdims=True))
        alpha = jnp.exp(m_prev - m_new)
        p = jnp.exp(s - m_new)
        l_ref[...] = alpha * l_ref[...] + jnp.sum(p, axis=0, keepdims=True)
        acc_ref[...] = alpha * acc_ref[...] + _dot(ckt_ref[kb], p.astype(BF16))
        m_ref[...] = m_new

    nfull = (qi * tq) // tk

    def full_block(kb, carry):
        block(kb, False)
        return carry

    lax.fori_loop(0, nfull, full_block, 0)
    block(nfull, True)
    ot = acc_ref[...] / l_ref[...]
    o = jnp.concatenate([ot[:, h * tq:(h + 1) * tq].T for h in range(H_A)], axis=0)
    o_ref[...] = _uv_project(o, wuv_ref, tq).astype(o_ref.dtype)


def _mla_prompt_attn(qlat, qpe, ckv, kpet, wuv, b, t):
    tq = min(256, t)
    tk = min(256, t)
    nq = t // tq
    cols = H_A * tq
    return pl.pallas_call(
        functools.partial(_mla_prompt_body, tq=tq, tk=tk),
        grid=(b, nq),
        in_specs=[pl.BlockSpec((H_A, KV_LORA, tq), lambda bi, qi: (0, 0, bi * nq + qi)),
                  pl.BlockSpec((ROPE_W, tq), lambda bi, qi: (0, bi * nq + qi)),
                  pl.BlockSpec((t, KV_LORA), lambda bi, qi: (bi, 0)),
                  pl.BlockSpec((t, ROPE_W), lambda bi, qi: (bi, 0)),
                  pl.BlockSpec(wuv.shape, lambda bi, qi: (0, 0, 0))],
        out_specs=pl.BlockSpec((tq, H_A * V_A), lambda bi, qi: (bi * nq + qi, 0)),
        out_shape=jax.ShapeDtypeStruct((b * t, H_A * V_A), BF16),
        scratch_shapes=[pltpu.VMEM((t, KV_LORA), BF16),
                        pltpu.VMEM((t // tk, KV_LORA, tk), BF16),
                        pltpu.VMEM((1, cols), F32),
                        pltpu.VMEM((1, cols), F32),
                        pltpu.VMEM((KV_LORA, cols), F32)],
        compiler_params=_cparams("parallel", "arbitrary"),
        name="mla_prompt_attn",
    )(qlat, qpe, ckv, kpet, wuv)


def _page_copies(pt_ref, first, stride, g, caches, bufs, sems, slot):
    copies = []
    for i in range(g):
        page = pt_ref[first + stride * i]
        for k, (cache, buf) in enumerate(zip(caches, bufs)):
            copies.append(pltpu.make_async_copy(cache.at[page], buf.at[slot, i], sems.at[k, slot]))
    return copies


def _paged_step(pt_ref, g, caches, bufs, sems, first_of):
    ns = pl.num_programs(1)
    lin = pl.program_id(0) * ns + pl.program_id(1)
    total = pl.num_programs(0) * ns
    slot = lin % 2

    @pl.when(lin == 0)
    def _():
        for cp in _page_copies(pt_ref, *first_of(lin), g, caches, bufs, sems, slot):
            cp.start()

    @pl.when(lin + 1 < total)
    def _():
        for cp in _page_copies(pt_ref, *first_of(lin + 1), g, caches, bufs, sems, 1 - slot):
            cp.start()

    for cp in _page_copies(pt_ref, *first_of(lin), g, caches, bufs, sems, slot):
        cp.wait()
    return slot


def _mla_sample_body(pt_ref, qlat_ref, qpe_ref, ckn_ref, kpn_ref, wuv_ref, ck_hbm, kp_hbm,
                     o_ref, ql_ref, qp_ref, ckpad_ref, kppad_ref, m_ref, l_ref, acc_ref,
                     ckbuf, kpbuf, sems, *, tn, g, nchain):
    step = pl.program_id(1)
    rows = H_A * tn
    slot = _paged_step(pt_ref, g, (ck_hbm, kp_hbm), (ckbuf, kpbuf), sems, lambda n: (n * g, 1))

    def update(s, ck, carry):
        m_prev, l_prev, acc = carry
        m_new = jnp.maximum(m_prev, jnp.max(s, axis=-1, keepdims=True))
        alpha = jnp.exp(m_prev - m_new)
        p = jnp.exp(s - m_new)
        l_new = alpha * l_prev + jnp.sum(p, axis=-1, keepdims=True)
        return m_new, l_new, alpha * acc + _dot(p.astype(BF16), ck)

    @pl.when(step == 0)
    def _():
        ql_ref[...] = qlat_ref[...].reshape(rows, KV_LORA).astype(BF16)
        qps = _head_stack(qpe_ref[...], ROPE_A, H_A).astype(BF16)
        fr = lax.broadcasted_iota(jnp.int32, (ROPE_W, ROPE_A), 0)
        fc = lax.broadcasted_iota(jnp.int32, (ROPE_W, ROPE_A), 1)
        fold = jnp.where((fr & (ROPE_A - 1)) == fc, 1.0, 0.0).astype(BF16)
        qp_ref[...] = _dot(qps, fold).astype(BF16)
        ckpad_ref[...] = jnp.zeros(ckpad_ref.shape, F32)
        kppad_ref[...] = jnp.zeros(kppad_ref.shape, F32)
        ckpad_ref[0:tn, :] = ckn_ref[...]
        kppad_ref[0:tn, :] = kpn_ref[...]
        row = lax.broadcasted_iota(jnp.int32, (rows, PAGE), 0) & (tn - 1)
        col = lax.broadcasted_iota(jnp.int32, (rows, PAGE), 1)
        ck = ckpad_ref[...].astype(BF16)
        s = _dot_nt(ql_ref[...], ck) + _dot_nt(qp_ref[...], kppad_ref[...].astype(BF16))
        s = jnp.where(col <= row, s, -jnp.inf)
        init = (jnp.full((rows, 1), -jnp.inf, F32), jnp.zeros((rows, 1), F32),
                jnp.zeros((rows, KV_LORA), F32))
        m_ref[0], l_ref[0], acc_ref[0] = update(s, ck, init)
        for c in range(1, nchain):
            m_ref[c], l_ref[c], acc_ref[c] = init

    per = g // nchain
    for c in range(nchain):
        pages = range(c * per, (c + 1) * per)
        ck = jnp.concatenate([ckbuf[slot, i] for i in pages], axis=0).astype(BF16)
        kp = jnp.concatenate([kpbuf[slot, i] for i in pages], axis=1).astype(BF16)
        s = _dot_nt(ql_ref[...], ck) + _dot(qp_ref[...], kp)
        m_ref[c], l_ref[c], acc_ref[c] = update(s, ck, (m_ref[c], l_ref[c], acc_ref[c]))

    @pl.when(step == pl.num_programs(1) - 1)
    def _():
        m = m_ref[0]
        for c in range(1, nchain):
            m = jnp.maximum(m, m_ref[c])
        l = jnp.zeros((rows, 1), F32)
        acc = jnp.zeros((rows, KV_LORA), F32)
        for c in range(nchain):
            w = jnp.exp(m_ref[c] - m)
            l = l + w * l_ref[c]
            acc = acc + w * acc_ref[c]
        o_ref[...] = _uv_project(acc / l, wuv_ref, tn).astype(o_ref.dtype)


def _mla_sample_attn(page_table, qlat, qpe, ckn, kpn, wuv, cache_ckv, cache_kpe, tn, g):
    bs, n_pages = page_table.shape
    rows = H_A * tn
    pt = page_table.reshape(-1)
    nchain = 2 if g % 2 == 0 else 1
    in_specs = [pl.BlockSpec((H_A, tn, KV_LORA), lambda b, s, pt_ref: (0, b, 0)),
                pl.BlockSpec((tn, ROPE_W), lambda b, s, pt_ref: (b, 0)),
                pl.BlockSpec((tn, KV_LORA), lambda b, s, pt_ref: (b, 0)),
                pl.BlockSpec((tn, ROPE_A), lambda b, s, pt_ref: (b, 0)),
                pl.BlockSpec(wuv.shape, lambda b, s, pt_ref: (0, 0, 0)),
                pl.BlockSpec(memory_space=pl.ANY),
                pl.BlockSpec(memory_space=pl.ANY)]
    return pl.pallas_call(
        functools.partial(_mla_sample_body, tn=tn, g=g, nchain=nchain),
        grid_spec=pltpu.PrefetchScalarGridSpec(
            num_scalar_prefetch=1,
            grid=(bs, n_pages // g),
            in_specs=in_specs,
            out_specs=pl.BlockSpec((tn, H_A * V_A), lambda b, s, pt_ref: (b, 0)),
            scratch_shapes=[pltpu.VMEM((rows, KV_LORA), BF16),
                            pltpu.VMEM((rows, ROPE_A), BF16),
                            pltpu.VMEM((PAGE, KV_LORA), F32),
                            pltpu.VMEM((PAGE, ROPE_A), F32),
                            pltpu.VMEM((nchain, rows, 1), F32),
                            pltpu.VMEM((nchain, rows, 1), F32),
                            pltpu.VMEM((nchain, rows, KV_LORA), F32),
                            pltpu.VMEM((2, g, PAGE, KV_LORA), F32),
                            pltpu.VMEM((2, g, ROPE_A, PAGE), F32),
                            pltpu.SemaphoreType.DMA((2, 2))]),
        out_shape=jax.ShapeDtypeStruct((bs * tn, H_A * V_A), F32),
        compiler_params=_cparams("arbitrary", "arbitrary"),
        name="mla_sample_attn",
    )(pt, qlat, qpe, ckn, kpn, wuv, cache_ckv, cache_kpe)


def _sb_block(qs, k, v, rest, acc, upper, mask=None):
    z = _dot_nt(qs, k)
    l_neg = _log_sigmoid(-z)
    if mask is not None:
        l_neg = jnp.where(mask, l_neg, 0.0)
    hi, lo = _split_bf16(l_neg, 2)
    later = _dot(hi, upper) + _dot(lo, upper) + rest
    a = jnp.exp(z + l_neg + later)
    if mask is not None:
        a = jnp.where(mask, a, 0.0)
    acc = acc + _dot(a.astype(BF16), v)
    rest = rest + jnp.sum(l_neg, axis=-1, keepdims=True)
    return rest, acc


def _upper_ones(n):
    r = lax.broadcasted_iota(jnp.int32, (n, n), 0)
    c = lax.broadcasted_iota(jnp.int32, (n, n), 1)
    return jnp.where(r > c, 1.0, 0.0).astype(BF16)


def _head_collapse(acc, t, width, n_heads):
    lane = lax.broadcasted_iota(jnp.int32, (t, n_heads * width), 1)
    out = jnp.zeros((t, n_heads * width), F32)
    for h in range(n_heads):
        out = out + jnp.where((lane >= h * width) & (lane < (h + 1) * width), acc[h * t:(h + 1) * t], 0.0)
    return out


def _sb_prompt_body(q_ref, k_ref, v_ref, o_ref, kb_ref, vb_ref, rest_ref, acc_ref, *, tq, tk):
    qi = pl.program_id(1)
    rows = H_D * tq

    @pl.when(qi == 0)
    def _():
        kb_ref[...] = k_ref[...].astype(BF16)
        vb_ref[...] = v_ref[...].astype(BF16)

    qs = _head_stack(q_ref[...] * SB_SCALE, DH_D, H_D).astype(BF16)
    upper = _upper_ones(tk)
    rest_ref[...] = jnp.zeros(rest_ref.shape, F32)
    acc_ref[...] = jnp.zeros(acc_ref.shape, F32)

    def block(k0, masked):
        mask = None
        if masked:
            row = lax.broadcasted_iota(jnp.int32, (rows, tk), 0) & (tq - 1)
            col = lax.broadcasted_iota(jnp.int32, (rows, tk), 1)
            mask = col + k0 < row + qi * tq
        rest, acc = _sb_block(qs, kb_ref[pl.ds(k0, tk), :], vb_ref[pl.ds(k0, tk), :],
                              rest_ref[...], acc_ref[...], upper, mask)
        rest_ref[...] = rest
        acc_ref[...] = acc

    nfull = (qi * tq) // tk
    block(pl.multiple_of(nfull * tk, tk), True)

    def full_block(i, carry):
        block(pl.multiple_of((nfull - 1 - i) * tk, tk), False)
        return carry

    lax.fori_loop(0, nfull, full_block, 0)
    o_ref[...] = _head_collapse(acc_ref[...], tq, DH_D, H_D).astype(o_ref.dtype)


def _sb_prompt_attn(q, k, v, b, t):
    tq = min(256, t)
    tk = min(256, t)
    nq = t // tq
    rows = H_D * tq
    w = H_D * DH_D
    return pl.pallas_call(
        functools.partial(_sb_prompt_body, tq=tq, tk=tk),
        grid=(b, nq),
        in_specs=[pl.BlockSpec((tq, w), lambda bi, qi: (bi * nq + qi, 0)),
                  pl.BlockSpec((t, w), lambda bi, qi: (bi, 0)),
                  pl.BlockSpec((t, w), lambda bi, qi: (bi, 0))],
        out_specs=pl.BlockSpec((tq, w), lambda bi, qi: (bi * nq + qi, 0)),
        out_shape=jax.ShapeDtypeStruct((b * t, w), BF16),
        scratch_shapes=[pltpu.VMEM((t, w), BF16),
                        pltpu.VMEM((t, w), BF16),
                        pltpu.VMEM((rows, 1), F32),
                        pltpu.VMEM((rows, w), F32)],
        compiler_params=_cparams("parallel", "arbitrary"),
        name="sb_prompt_attn",
    )(q, k, v)


def _sb_sample_body(pt_ref, q_ref, kn_ref, vn_ref, k_hbm, v_hbm, o_ref, qs_ref, kpad_ref, vpad_ref,
                    rest_ref, acc_ref, kbuf, vbuf, sems, *, tn, g, ns):
    step = pl.program_id(1)
    n_pages = ns * g

    def newest_first(n):
        seq, s = n // ns, n % ns
        return seq * n_pages + n_pages - 1 - s * g, -1

    slot = _paged_step(pt_ref, g, (k_hbm, v_hbm), (kbuf, vbuf), sems, newest_first)
    rows = H_D * tn
    r = lax.broadcasted_iota(jnp.int32, (PAGE, 2 * PAGE), 0)
    c = lax.broadcasted_iota(jnp.int32, (PAGE, 2 * PAGE), 1)
    sums = jnp.where((c >= PAGE) | (r > c), 1.0, 0.0).astype(BF16)

    def page_sums(l_negs):
        split = [_split_bf16(x, 2) for x in l_negs]
        stacked = jnp.concatenate([s[0] for s in split] + [s[1] for s in split], axis=0)
        cs = _dot(stacked, sums)
        n = len(l_negs) * rows
        cs = cs[:n] + cs[n:]
        return [(cs[i * rows:(i + 1) * rows, :PAGE], cs[i * rows:(i + 1) * rows, PAGE:])
                for i in range(len(l_negs))]

    @pl.when(step == 0)
    def _():
        qs_ref[...] = _head_stack(q_ref[...] * SB_SCALE, DH_D, H_D).astype(BF16)
        kpad_ref[...] = jnp.zeros(kpad_ref.shape, F32)
        vpad_ref[...] = jnp.zeros(vpad_ref.shape, F32)
        kpad_ref[0:tn, :] = kn_ref[...]
        vpad_ref[0:tn, :] = vn_ref[...]
        row = lax.broadcasted_iota(jnp.int32, (rows, PAGE), 0) & (tn - 1)
        col = lax.broadcasted_iota(jnp.int32, (rows, PAGE), 1)
        mask = col < row
        z = _dot_nt(qs_ref[...], kpad_ref[...].astype(BF16))
        l_neg = jnp.where(mask, _log_sigmoid(-z), 0.0)
        (later, total), = page_sums([l_neg])
        a = jnp.where(mask, jnp.exp(z + l_neg + later), 0.0)
        acc_ref[...] = _dot(a.astype(BF16), vpad_ref[...].astype(BF16))
        rest_ref[...] = total

    kt = jnp.concatenate([kbuf[slot, i] for i in range(g)], axis=1).astype(BF16)
    vt = jnp.concatenate([vbuf[slot, i] for i in range(g)], axis=1).astype(BF16)
    z = _dot(qs_ref[...], kt)
    l_neg = _log_sigmoid(-z)
    zl = z + l_neg
    stick = rest_ref[...]
    logits = []
    for i, (later, total) in enumerate(page_sums([l_neg[:, i * PAGE:(i + 1) * PAGE] for i in range(g)])):
        logits.append(zl[:, i * PAGE:(i + 1) * PAGE] + later + stick)
        stick = stick + total
    a = jnp.exp(jnp.concatenate(logits, axis=1)).astype(BF16)
    acc_ref[...] += _dot_nt(a, vt)
    rest_ref[...] = stick

    @pl.when(step == pl.num_programs(1) - 1)
    def _():
        o_ref[...] = _head_collapse(acc_ref[...], tn, DH_D, H_D).astype(o_ref.dtype)


def _sb_sample_attn(page_table, q, kn, vn, cache_k, cache_v, tn, g):
    bs, n_pages = page_table.shape
    rows = H_D * tn
    w = H_D * DH_D
    pt = page_table.reshape(-1)

    in_specs = [pl.BlockSpec((tn, w), lambda b, s, pt_ref: (b, 0))] * 3
    in_specs += [pl.BlockSpec(memory_space=pl.ANY)] * 2
    return pl.pallas_call(
        functools.partial(_sb_sample_body, tn=tn, g=g, ns=n_pages // g),
        grid_spec=pltpu.PrefetchScalarGridSpec(
            num_scalar_prefetch=1,
            grid=(bs, n_pages // g),
            in_specs=in_specs,
            out_specs=pl.BlockSpec((tn, w), lambda b, s, pt_ref: (b, 0)),
            scratch_shapes=[pltpu.VMEM((rows, w), BF16),
                            pltpu.VMEM((PAGE, w), F32),
                            pltpu.VMEM((PAGE, w), F32),
                            pltpu.VMEM((rows, PAGE), F32),
                            pltpu.VMEM((rows, w), F32),
                            pltpu.VMEM((2, g, w, PAGE), F32),
                            pltpu.VMEM((2, g, w, PAGE), F32),
                            pltpu.SemaphoreType.DMA((2, 2))]),
        out_shape=jax.ShapeDtypeStruct((bs * tn, w), F32),
        compiler_params=_cparams("arbitrary", "arbitrary"),
        name="sb_sample_attn",
    )(pt, q, kn, vn, cache_k, cache_v)


def _gated_body(*refs, mode, lq, nblk, ngrp, dk, dv, dkp, dvp, has_s0):
    nh = 4
    nsq = CHUNK // lq
    it = iter(refs)
    if mode == "hgrn":
        zq_ref, zf_ref, zv_ref, zg_ref, lb_ref, nw_ref = (next(it) for _ in range(6))
    else:
        zq_ref, zk_ref, zv_ref, zg_ref, ga_ref, wg_hi_ref, wg_lo_ref, bg_ref, nw_ref = (next(it) for _ in range(9))
    s0_ref = next(it) if has_s0 else None
    o_ref, sout_ref, s_ref = next(it), next(it), next(it)
    step = pl.program_id(1)

    if has_s0:
        s_ref[...] = jnp.zeros(s_ref.shape, F32)
        s_ref[:, :, 0:dk, 0:dv] = s0_ref[...]
    else:
        @pl.when(step == 0)
        def _():
            s_ref[...] = jnp.zeros(s_ref.shape, F32)

    r = lax.broadcasted_iota(jnp.int32, (CHUNK, CHUNK), 0)
    c = lax.broadcasted_iota(jnp.int32, (CHUNK, CHUNK), 1)
    seq_r, seq_c = r // lq, c // lq
    same = seq_r == seq_c
    causal = same & (r >= c)
    ref_pos = seq_r * lq + lq // 2
    ones = lambda m: jnp.where(m, 1.0, 0.0).astype(BF16)
    sums = jnp.concatenate([ones(causal), ones(same & (c <= ref_pos)), ones(same)], axis=0)
    row_seq = lax.broadcasted_iota(jnp.int32, (CHUNK, dkp), 0) // lq

    def chunk(gi, ci):
        rows = pl.ds(pl.multiple_of(ci * CHUNK, CHUNK), CHUNK)
        if mode == "hgrn":
            lb = lb_ref[...]
            f = lb + (1.0 - lb) * _sigmoid(zf_ref[gi, rows, :])
            logf, k = jnp.log(f), 1.0 - f
            zq = zq_ref[gi, rows, :].astype(F32)
            q = zq * _sigmoid(zq)
        else:
            ga_hi, ga_lo = _split_bf16(ga_ref[gi, rows, :], 2)
            logits = (_dot(ga_hi, wg_hi_ref[...]) + _dot(ga_hi, wg_lo_ref[...])
                      + _dot(ga_lo, wg_hi_ref[...]) + bg_ref[...])
            logf = _log_sigmoid(logits) * (1.0 / GATE_NORM)
            k = zk_ref[gi, rows, :].astype(F32)
            q = zq_ref[gi, rows, :].astype(F32) * (dk ** -0.5)
        v = zv_ref[gi, rows, :].astype(BF16)
        parts = _split_bf16(logf, 3)
        gall = _dot(sums, parts[0]) + _dot(sums, parts[1]) + _dot(sums, parts[2])
        g, gr, gl = gall[:CHUNK], gall[CHUNK:2 * CHUNK], gall[2 * CHUNK:]
        qg = (q * jnp.exp(g - gr)).astype(BF16)
        kg = (k * jnp.exp(gr - g)).astype(BF16)
        qs = q * jnp.exp(g)
        kd = k * jnp.exp(gl - g)
        el = jnp.exp(gl)
        gate = zg_ref[gi, rows, :].astype(F32)
        outs = []
        for h in range(nh):
            ks, vs = slice(h * dkp, (h + 1) * dkp), slice(h * dvp, (h + 1) * dvp)
            a = jnp.where(causal, _dot_nt(qg[:, ks], kg[:, ks]), 0.0)
            o = _dot(a.astype(BF16), v[:, vs])
            for s in range(nsq):
                if nsq == 1:
                    qs_s, kd_s = qs[:, ks], kd[:, ks]
                else:
                    qs_s = jnp.where(row_seq == s, qs[:, ks], 0.0)
                    kd_s = jnp.where(row_seq == s, kd[:, ks], 0.0)
                st = s_ref[gi * nsq + s, h]
                o = o + _dot(qs_s.astype(BF16), st.astype(BF16))
                e_col = jnp.broadcast_to(el[s * lq:s * lq + 1, ks], (dkp, dkp)).T
                if dvp > dkp:
                    e_col = jnp.concatenate([e_col] * (dvp // dkp), axis=1)
                s_ref[gi * nsq + s, h] = e_col * st + _dot_tn(kd_s.astype(BF16), v[:, vs])
            nwh = nw_ref[...]
            gh = gate[:, vs]
            outs.append(_rms(o, nwh, dv) * (gh * _sigmoid(gh)))
        o_ref[gi, rows, :] = jnp.concatenate(outs, axis=-1).astype(o_ref.dtype)

    if nblk == 1:
        for gi in range(ngrp):
            chunk(gi, 0)
    else:
        def loop_body(ci, carry):
            for gi in range(ngrp):
                chunk(gi, ci)
            return carry
        lax.fori_loop(0, nblk, loop_body, 0)

    if has_s0:
        sout_ref[...] = s_ref[:, :, 0:dk, 0:dv]
    else:
        @pl.when(step == pl.num_programs(1) - 1)
        def _():
            sout_ref[...] = s_ref[:, :, 0:dk, 0:dv]


def _gated_linear(mode, arrays, col_blocks, params, s0, b, t, dk, dv, dkp, dvp):
    nh = 4
    m = b * t
    lq = min(CHUNK, t)
    nsq = CHUNK // lq
    if s0 is None:
        n_groups, group_rows = b, t
    else:
        n_groups, group_rows = m // CHUNK, CHUNK
    ngrp = max(n for n in ((4, 2, 1) if s0 is None else (2, 1)) if n_groups % n == 0)
    nblk = min(16 // ngrp, group_rows // CHUNK)
    tr = CHUNK * nblk
    grid = (n_groups // ngrp, group_rows // tr)
    state_map = lambda bi, si: (bi, 0, 0, 0)
    arrays = [a.reshape(n_groups, group_rows, a.shape[-1]) for a in arrays]
    in_specs = [pl.BlockSpec((ngrp, tr, w), functools.partial(lambda bi, si, cb: (bi, si, cb), cb=cb))
                for (w, cb) in col_blocks]
    in_specs += [pl.BlockSpec(p.shape, lambda bi, si: (0, 0)) for p in params]
    operands = list(arrays) + list(params)
    if s0 is not None:
        in_specs.append(pl.BlockSpec((ngrp * nsq, nh, dk, dv), state_map))
        operands.append(s0)
    o, s_out = pl.pallas_call(
        functools.partial(_gated_body, mode=mode, lq=lq, nblk=nblk, ngrp=ngrp, dk=dk, dv=dv, dkp=dkp,
                          dvp=dvp, has_s0=s0 is not None),
        grid=grid,
        in_specs=in_specs,
        out_specs=[pl.BlockSpec((ngrp, tr, nh * dvp), lambda bi, si: (bi, si, 0)),
                   pl.BlockSpec((ngrp * nsq, nh, dk, dv), state_map)],
        out_shape=[jax.ShapeDtypeStruct((n_groups, group_rows, nh * dvp), BF16),
                   jax.ShapeDtypeStruct((b, nh, dk, dv), F32)],
        scratch_shapes=[pltpu.VMEM((ngrp * nsq, nh, dkp, dvp), F32)],
        compiler_params=_cparams("parallel", "arbitrary"),
        name="gated_linear_" + mode,
    )(*operands)
    return o.reshape(m, nh * dvp), s_out


def _ffn_body(x_ref, oa_ref, ob_ref, woa_ref, wob_ref, fnw_ref, wg_ref, wu_ref, wo_ref, finw_ref,
              y_ref, h_ref, xn_ref, acc_ref, *, final_norm):
    j = pl.program_id(1)

    @pl.when(j == 0)
    def _():
        h = (x_ref[...] + _dot(oa_ref[...].astype(BF16), woa_ref[...])
             + _dot(ob_ref[...].astype(BF16), wob_ref[...]))
        h_ref[...] = h
        xn_ref[...] = _rms(h, fnw_ref[...]).astype(BF16)
        acc_ref[...] = jnp.zeros(acc_ref.shape, F32)

    xn = xn_ref[...]
    gate = _dot(xn, wg_ref[...])
    up = _dot(xn, wu_ref[...])
    act = (gate * _sigmoid(gate) * up).astype(BF16)
    acc_ref[...] += _dot(act, wo_ref[...])

    @pl.when(j == pl.num_programs(1) - 1)
    def _():
        y = h_ref[...] + acc_ref[...]
        if final_norm:
            y = _rms(y, finw_ref[...])
        y_ref[...] = y


def _mix_ffn(x, oa, ob, woa, wob, fnw, w_in, w_out, finw, final_norm):
    m = x.shape[0]
    tm = min(512, m)
    tf = D_FF // 2
    nf = D_FF // tf
    ka, kb = oa.shape[1], ob.shape[1]
    return pl.pallas_call(
        functools.partial(_ffn_body, final_norm=final_norm),
        grid=(m // tm, nf),
        in_specs=[pl.BlockSpec((tm, D_MODEL), lambda i, j: (i, 0)),
                  pl.BlockSpec((tm, ka), lambda i, j: (i, 0)),
                  pl.BlockSpec((tm, kb), lambda i, j: (i, 0)),
                  pl.BlockSpec((ka, D_MODEL), lambda i, j: (0, 0)),
                  pl.BlockSpec((kb, D_MODEL), lambda i, j: (0, 0)),
                  pl.BlockSpec((1, D_MODEL), lambda i, j: (0, 0)),
                  pl.BlockSpec((D_MODEL, tf), lambda i, j: (0, j)),
                  pl.BlockSpec((D_MODEL, tf), lambda i, j: (0, j + nf)),
                  pl.BlockSpec((tf, D_MODEL), lambda i, j: (j, 0)),
                  pl.BlockSpec((1, D_MODEL), lambda i, j: (0, 0))],
        out_specs=pl.BlockSpec((tm, D_MODEL), lambda i, j: (i, 0)),
        out_shape=jax.ShapeDtypeStruct((m, D_MODEL), F32),
        scratch_shapes=[pltpu.VMEM((tm, D_MODEL), F32),
                        pltpu.VMEM((tm, D_MODEL), BF16),
                        pltpu.VMEM((tm, D_MODEL), F32)],
        compiler_params=_cparams("parallel", "arbitrary"),
        name="mix_ffn",
    )(x, oa, ob, woa, wob, fnw.reshape(1, -1), w_in, w_in, w_out, finw.reshape(1, -1))


def _pad_heads(w, nh, d, dp, axis):
    shape = w.shape
    w = w.reshape(shape[:axis] + (nh, d) + shape[axis + 1:])
    pad = [(0, 0)] * w.ndim
    pad[axis + 1] = (0, dp - d)
    w = jnp.pad(w, pad)
    return w.reshape(shape[:axis] + (nh * dp,) + shape[axis + 1:])


def _rope_tables(pos):
    inv = ROPE_THETA ** (-jnp.arange(0, ROPE_A, 2, dtype=F32) / ROPE_A)
    ang = pos[:, None] * inv[None, :]
    cos = jnp.repeat(jnp.cos(ang), 2, axis=1)
    sin = jnp.repeat(jnp.sin(ang), 2, axis=1) * jnp.tile(jnp.array([-1.0, 1.0], F32), ROPE_A // 2)[None, :]
    return jnp.tile(cos, (1, H_A)), jnp.tile(sin, (1, H_A))


def _prepare_weights(P):
    W = {}
    w_ab = P["w_in_ab"][0]
    o_ckv, o_kpe, o_h = Q_LORA, Q_LORA + KV_LORA, Q_LORA + KV_LORA + ROPE_A
    W["w_mla"] = jnp.concatenate(
        [w_ab[:, :o_kpe], jnp.tile(w_ab[:, o_kpe:o_h], (1, H_A))], axis=1).astype(BF16)
    W["w_hgrn"] = w_ab[:, o_h:].astype(BF16)
    wuq = P["mla_w_uq"][0].reshape(Q_LORA, H_A, NOPE_A + ROPE_A)
    W["w_uq"] = jnp.concatenate([wuq[:, :, :NOPE_A].reshape(Q_LORA, -1),
                                 wuq[:, :, NOPE_A:].reshape(Q_LORA, -1)], axis=1).astype(BF16)
    half = (jnp.arange(H_A) % 2)[:, None, None]
    wk = jnp.transpose(P["mla_w_uk"][0], (1, 2, 0))
    W["w_uk"] = jnp.where(half == 0, jnp.pad(wk, ((0, 0), (0, NOPE_A), (0, 0))),
                          jnp.pad(wk, ((0, 0), (NOPE_A, 0), (0, 0)))).astype(BF16)
    W["w_uk_t"] = jnp.transpose(W["w_uk"], (0, 2, 1))
    wv = jnp.transpose(P["mla_w_uv"][0], (1, 0, 2))
    W["w_uv"] = jnp.where(half == 0, jnp.pad(wv, ((0, 0), (0, 0), (0, V_A))),
                          jnp.pad(wv, ((0, 0), (0, 0), (V_A, 0)))).astype(BF16)
    lb = jnp.cumsum(jax.nn.softmax(P["hgrn_lb_logits"].astype(F32), axis=0), axis=0)[0]
    W["hgrn_lb"] = lb.reshape(1, -1)
    W["hgrn_nw"] = P["hgrn_norm_w"][0].reshape(1, -1)
    w_out_ab = P["w_out_ab"][0].astype(BF16)
    W["w_out_a"], W["w_out_b"] = w_out_ab[:H_A * V_A], w_out_ab[H_A * V_A:]

    w_cd = P["w_in_cd"][0]
    offs = [0]
    for wdt in (H_C * DK_C, H_C * DK_C, H_C * DV_C, H_C * DV_C, GATE_RANK, H_D * DH_D * 3):
        offs.append(offs[-1] + wdt)
    seg = lambda i: w_cd[:, offs[i]:offs[i + 1]]
    W["w_gla"] = jnp.concatenate(
        [_pad_heads(seg(0), H_C, DK_C, DKP_C, 1), _pad_heads(seg(1), H_C, DK_C, DKP_C, 1),
         _pad_heads(seg(2), H_C, DV_C, DVP_C, 1), _pad_heads(seg(3), H_C, DV_C, DVP_C, 1),
         jnp.pad(seg(4), ((0, 0), (0, H_C * DKP_C - GATE_RANK)))], axis=1).astype(BF16)
    W["w_sb"] = seg(5).astype(BF16)
    wg2 = jnp.pad(_pad_heads(P["gla_w_gate2"][0], H_C, DK_C, DKP_C, 1), ((0, GATE_RANK_P - GATE_RANK), (0, 0)))
    W["wg2_hi"] = wg2.astype(BF16)
    W["wg2_lo"] = (wg2 - W["wg2_hi"].astype(F32)).astype(BF16)
    W["bg2"] = _pad_heads(P["gla_b_gate2"][0].reshape(1, -1), H_C, DK_C, DKP_C, 1)
    W["gla_nw"] = jnp.pad(P["gla_norm_w"][0], (0, DVP_C - DV_C)).reshape(1, -1)
    w_out_cd = P["w_out_cd"][0]
    W["w_out_c"] = _pad_heads(w_out_cd[:H_C * DV_C], H_C, DV_C, DVP_C, 0).astype(BF16)
    W["w_out_d"] = w_out_cd[H_C * DV_C:].astype(BF16)
    W["w_ffn_in"] = [P["w_ffn_in"][l].astype(BF16) for l in range(2)]
    W["w_ffn_out"] = [P["w_ffn_out"][l].astype(BF16) for l in range(2)]
    return W


def _trunk(x3, pos, P, W, ctx):
    b, t, _ = x3.shape
    m = b * t
    x = x3.reshape(m, D_MODEL)
    sample = ctx is not None
    cos, sin = _rope_tables(pos)
    reps = min(512, m) // t
    if reps > 1:
        cos, sin = jnp.tile(cos, (reps, 1)), jnp.tile(sin, (reps, 1))

    qlat, qpe, ckv, kpe, kpet = _mla_prep(
        x, P["attn_norm_w"][0], W["w_mla"], P["mla_q_norm_w"][0], W["w_uq"],
        W["w_uk"] if sample else W["w_uk_t"], P["mla_kv_norm_w"][0], cos, sin,
        F32 if sample else BF16, transposed_q=not sample)
    zh = _norm_matmul(x, P["attn_norm_w"][0], W["w_hgrn"], H_B * DK_B,
                      [(1, BF16), (1, F32), (1, BF16), (1, BF16)])
    if sample:
        o_a = _mla_sample_attn(ctx["page_table"], qlat, qpe, ckv, kpe, W["w_uv"],
                               ctx["cache_mla_ckv"], ctx["cache_mla_kpe"], t, ctx["pages_per_step"])
        s0 = ctx["state_hgrn"][0]
    else:
        o_a = _mla_prompt_attn(qlat, qpe, ckv, kpet, W["w_uv"], b, t)
        s0 = None
    wb = H_B * DK_B
    o_b, s_hgrn = _gated_linear("hgrn", zh, [(wb, 0)] * 4, [W["hgrn_lb"], W["hgrn_nw"]], s0,
                                b, t, DK_B, DV_B, DK_B, DV_B)
    h1 = _mix_ffn(x, o_a, o_b, W["w_out_a"], W["w_out_b"], P["ffn_norm_w"][0],
                  W["w_ffn_in"][0], W["w_ffn_out"][0], P["final_norm_w"], final_norm=False)

    wq, wv = H_C * DKP_C, H_C * DVP_C
    zg, zga = _norm_matmul(h1, P["attn_norm_w"][1], W["w_gla"], wq,
                           [((2 * wq + 2 * wv) // wq, BF16), (1, F32)])
    sq, sk, sv = _norm_matmul(h1, P["attn_norm_w"][1], W["w_sb"], H_D * DH_D, [(1, F32)] * 3)
    if sample:
        o_d = _sb_sample_attn(ctx["page_table"], sq, sk, sv, ctx["cache_sb_k"], ctx["cache_sb_v"],
                              t, ctx["pages_per_step"])
        s0 = ctx["state_gla"][0]
    else:
        o_d = _sb_prompt_attn(sq, sk, sv, b, t)
        s0 = None
    wq, wv = H_C * DKP_C, H_C * DVP_C
    o_c, s_gla = _gated_linear(
        "gla", [zg] * 4 + [zga], [(wq, 0), (wq, 1), (wv, 1), (wv, 2), (GATE_RANK_P, 0)],
        [W["wg2_hi"], W["wg2_lo"], W["bg2"], W["gla_nw"]], s0, b, t, DK_C, DV_C, DKP_C, DVP_C)
    y = _mix_ffn(h1, o_c, o_d, W["w_out_c"], W["w_out_d"], P["ffn_norm_w"][1],
                 W["w_ffn_in"][1], W["w_ffn_out"][1], P["final_norm_w"], final_norm=True)

    new = (ckv.reshape(1, b, t, KV_LORA), kpe.reshape(1, b, t, ROPE_A), s_hgrn[None],
           sk.reshape(1, b, t, H_D, DH_D), sv.reshape(1, b, t, H_D, DH_D), s_gla[None])
    return y.reshape(b, t, D_MODEL), new


def kernel(x_prompt, x_sample, cache_mla_ckv, cache_mla_kpe, state_hgrn, cache_sb_k, cache_sb_v, state_gla, page_table, attn_norm_w, ffn_norm_w, final_norm_w, w_in_ab, mla_q_norm_w, mla_w_uq, mla_kv_norm_w, mla_w_uk, mla_w_uv, hgrn_lb_logits, hgrn_norm_w, w_out_ab, w_in_cd, gla_w_gate2, gla_b_gate2, gla_norm_w, w_out_cd, w_ffn_in, w_ffn_out):
    P = {
        "attn_norm_w": attn_norm_w, "ffn_norm_w": ffn_norm_w, "final_norm_w": final_norm_w,
        "w_in_ab": w_in_ab, "mla_q_norm_w": mla_q_norm_w, "mla_w_uq": mla_w_uq,
        "mla_kv_norm_w": mla_kv_norm_w, "mla_w_uk": mla_w_uk, "mla_w_uv": mla_w_uv,
        "hgrn_lb_logits": hgrn_lb_logits, "hgrn_norm_w": hgrn_norm_w, "w_out_ab": w_out_ab,
        "w_in_cd": w_in_cd, "gla_w_gate2": gla_w_gate2, "gla_b_gate2": gla_b_gate2,
        "gla_norm_w": gla_norm_w, "w_out_cd": w_out_cd, "w_ffn_in": w_ffn_in, "w_ffn_out": w_ffn_out,
    }
    W = _prepare_weights(P)
    n_pages = page_table.shape[1]
    past = n_pages * PAGE
    pos_prompt = jnp.arange(x_prompt.shape[1], dtype=F32)
    pos_sample = jnp.arange(x_sample.shape[1], dtype=F32) + float(past)
    n_pool = cache_sb_k.shape[1]
    ctx = {
        "page_table": page_table,
        "cache_mla_ckv": cache_mla_ckv[0],
        "cache_mla_kpe": jnp.transpose(cache_mla_kpe[0], (0, 2, 1)),
        "state_hgrn": state_hgrn,
        "cache_sb_k": jnp.transpose(cache_sb_k[0], (0, 2, 3, 1)).reshape(n_pool, H_D * DH_D, PAGE),
        "cache_sb_v": jnp.transpose(cache_sb_v[0], (0, 2, 3, 1)).reshape(n_pool, H_D * DH_D, PAGE),
        "state_gla": state_gla,
        "pages_per_step": min(32, n_pages),
    }
    y_p, new_p = _trunk(x_prompt, pos_prompt, P, W, None)
    y_s, new_s = _trunk(x_sample, pos_sample, P, W, ctx)
    return (y_p, y_s) + new_p + new_s
```
